```python
import math
import jax, jax.numpy as jnp
from jax import lax
import numpy as np

D_MODEL = 1024
BATCH = 8
SEQ = 2048
DEPTH = 1
DEC_BATCH = 128
DEC_SEQ = 1
PAST_LEN = 16384
PAGE_SIZE = 128

N_META = 16
D_LRU = D_MODEL
LRU_BLOCKS = 8
LRU_BLOCK = D_LRU // LRU_BLOCKS
LRU_CONV = 4
LRU_C = 8.0
D_SSM = D_MODEL // 2
SSM_GROUP = 16
SSM_GROUPS = D_SSM // SSM_GROUP
SSM_STATE = 64
D_FF = 3 * D_MODEL
FFN_CONV = 3
D_IN = D_LRU + D_SSM + 2 * D_MODEL
ALPHA = (2.0 * DEPTH) ** 0.25
BETA = (8.0 * DEPTH) ** -0.25
LN_EPS = 1e-5

kernel_name = 'hawk_s5_convffn_deepnorm_step'


def layer_norm(x, g, b):
    xf = x.astype(jnp.float32)
    mu = jnp.mean(xf, axis=-1, keepdims=True)
    var = jnp.mean(jnp.square(xf - mu), axis=-1, keepdims=True)
    return ((xf - mu) * lax.rsqrt(var + LN_EPS)).astype(x.dtype) * g + b


def causal_dwconv(x, buf, w, b):
    k = w.shape[0]
    t = x.shape[1]
    xp = jnp.concatenate([buf.astype(x.dtype), x], axis=1)
    y = b + sum(xp[:, j:j + t] * w[j] for j in range(k))
    return y, xp[:, t:]


def linear_scan(a, bx, h0):
    bx = bx.at[:, 0].add(a[:, 0] * h0)

    def comb(l, r):
        return r[0] * l[0], r[0] * l[1] + r[1]

    _, h = lax.associative_scan(comb, (a, bx), axis=1)
    return h


def complex_linear_scan(a_re, a_im, b_re, b_im, h0_re, h0_im):
    b_re = b_re.at[:, 0].add(a_re[:, 0] * h0_re - a_im[:, 0] * h0_im)
    b_im = b_im.at[:, 0].add(a_re[:, 0] * h0_im + a_im[:, 0] * h0_re)

    def comb(l, r):
        lar, lai, lbr, lbi = l
        rar, rai, rbr, rbi = r
        return (rar * lar - rai * lai, rar * lai + rai * lar,
                rar * lbr - rai * lbi + rbr, rar * lbi + rai * lbr + rbi)

    _, _, h_re, h_im = lax.associative_scan(comb, (a_re, a_im, b_re, b_im), axis=1)
    return h_re, h_im


def rg_lru(x, h0, w_r, b_r, w_i, b_i, lam):
    nb, t, _ = x.shape
    xb = x.reshape(nb, t, LRU_BLOCKS, LRU_BLOCK)
    r = jax.nn.sigmoid(jnp.einsum('btnc,ncd->btnd', xb, w_r).reshape(nb, t, D_LRU) + b_r)
    i = jax.nn.sigmoid(jnp.einsum('btnc,ncd->btnd', xb, w_i).reshape(nb, t, D_LRU) + b_i)
    log_a = (-LRU_C * r.astype(jnp.float32)) * jax.nn.softplus(-lam.astype(jnp.float32))
    a = jnp.exp(log_a)
    mult = jnp.sqrt(-jnp.expm1(2.0 * log_a))
    bx = mult * (i * x).astype(jnp.float32)
    h = linear_scan(a, bx, h0.astype(jnp.float32))
    return h.astype(x.dtype), h[:, -1]


def s5_ssm(u, h0_re, h0_im, lam_re, lam_im, log_dt, b_re, b_im, c_re, c_im, d):
    f32 = jnp.float32
    nb, t, _ = u.shape
    dt = jnp.exp(log_dt.astype(f32))[:, None]
    lre = lam_re.astype(f32)
    lim = lam_im.astype(f32)
    mag = jnp.exp(lre * dt)
    ab_re = mag * jnp.cos(lim * dt)
    ab_im = mag * jnp.sin(lim * dt)
    den = lre * lre + lim * lim
    nr = ab_re - 1.0
    ni = ab_im
    f_re = (nr * lre + ni * lim) / den
    f_im = (ni * lre - nr * lim) / den
    br = b_re.astype(f32)
    bi = b_im.astype(f32)
    bb_re = f_re[..., None] * br - f_im[..., None] * bi
    bb_im = f_re[..., None] * bi + f_im[..., None] * br
    ug = u.reshape(nb, t, SSM_GROUPS, SSM_GROUP).astype(f32)
    bu_re = jnp.einsum('btgc,gpc->btgp', ug, bb_re)
    bu_im = jnp.einsum('btgc,gpc->btgp', ug, bb_im)
    a_re = jnp.broadcast_to(ab_re, bu_re.shape)
    a_im = jnp.broadcast_to(ab_im, bu_re.shape)
    h_re, h_im = complex_linear_scan(a_re, a_im, bu_re, bu_im,
                                     h0_re.astype(f32), h0_im.astype(f32))
    y = (jnp.einsum('btgp,gcp->btgc', h_re, c_re.astype(f32))
         - jnp.einsum('btgp,gcp->btgc', h_im, c_im.astype(f32)))
    y = y.reshape(nb, t, D_SSM).astype(u.dtype) + d * u
    return y, h_re[:, -1], h_im[:, -1]


def hybrid_layer(x, conv_buf, h_lru, s_re, s_im, ffn_buf,
                 w_in, b_in, conv_lru_w, conv_lru_b, lru_w_r, lru_b_r, lru_w_i, lru_b_i, lru_lambda,
                 ssm_lambda_re, ssm_lambda_im, ssm_log_dt, ssm_b_re, ssm_b_im, ssm_c_re, ssm_c_im, ssm_d,
                 w_glu, b_glu, w_out, b_out, ln1_g, ln1_b,
                 w_up, b_up, ffn_conv_w, ffn_conv_b, w_down, b_down, ln2_g, ln2_b):
    proj = x @ w_in + b_in
    xa, us, ga, gb = jnp.split(proj, [D_LRU, D_LRU + D_SSM, D_LRU + D_SSM + D_MODEL], axis=-1)
    xa_c, new_conv = causal_dwconv(xa, conv_buf, conv_lru_w, conv_lru_b)
    ya, new_h = rg_lru(xa_c, h_lru, lru_w_r, lru_b_r, lru_w_i, lru_b_i, lru_lambda)
    ys, new_re, new_im = s5_ssm(us, s_re, s_im, ssm_lambda_re, ssm_lambda_im, ssm_log_dt,
                                ssm_b_re, ssm_b_im, ssm_c_re, ssm_c_im, ssm_d)
    glu = jax.nn.gelu(ys) @ w_glu + b_glu
    yb = glu[..., :D_MODEL] * jax.nn.sigmoid(glu[..., D_MODEL:])
    mix = jax.nn.sigmoid(ga) * ya + jax.nn.sigmoid(gb) * yb
    x = layer_norm(ALPHA * x + (mix @ w_out + b_out), ln1_g, ln1_b)
    up = x @ w_up + b_up
    g, v = up[..., :D_FF], up[..., D_FF:]
    g_c, new_ffn = causal_dwconv(g, ffn_buf, ffn_conv_w, ffn_conv_b)
    h = jax.nn.gelu(g_c) * v
    x = layer_norm(ALPHA * x + (h @ w_down + b_down), ln2_g, ln2_b)
    return x, (new_conv, new_h, new_re, new_im, new_ffn)


def setup_inputs(seed: int = 0) -> dict:
    key = jax.random.key(seed)
    ks = iter(jax.random.split(key, 48))

    def nrm(shape, scale):
        return jax.random.normal(next(ks), shape, jnp.float32) * scale

    u_a = jax.random.uniform(next(ks), (DEPTH, D_LRU), jnp.float32, 0.9, 0.999)
    a0 = u_a ** (1.0 / LRU_C)
    lru_lambda = jnp.log(a0) - jnp.log1p(-a0)
    n_idx = jnp.arange(SSM_STATE, dtype=jnp.float32)
    return {
        'x_prompt': nrm((BATCH, SEQ, D_MODEL), 1.0),
        'x_sample': nrm((DEC_BATCH, DEC_SEQ, D_MODEL), 1.0),
        'cache_conv_lru': nrm((DEPTH, DEC_BATCH, LRU_CONV - 1, D_LRU), 1.0),
        'state_lru': nrm((DEPTH, DEC_BATCH, D_LRU), 0.5),
        'state_ssm_re': nrm((DEPTH, DEC_BATCH, SSM_GROUPS, SSM_STATE), 0.1),
        'state_ssm_im': nrm((DEPTH, DEC_BATCH, SSM_GROUPS, SSM_STATE), 0.1),
        'cache_conv_ffn': nrm((DEPTH, DEC_BATCH, FFN_CONV - 1, D_FF), 1.0),
        'meta_tokens': nrm((N_META, D_MODEL), 1.0),
        'emb_ln_g': 1.0 + nrm((D_MODEL,), 0.01),
        'emb_ln_b': nrm((D_MODEL,), 0.01),
        'w_in': nrm((DEPTH, D_MODEL, D_IN), D_MODEL ** -0.5),
        'b_in': nrm((DEPTH, D_IN), 0.01),
        'conv_lru_w': nrm((DEPTH, LRU_CONV, D_LRU), LRU_CONV ** -0.5),
        'conv_lru_b': nrm((DEPTH, D_LRU), 0.01),
        'lru_w_r': nrm((DEPTH, LRU_BLOCKS, LRU_BLOCK, LRU_BLOCK), LRU_BLOCK ** -0.5),
        'lru_b_r': nrm((DEPTH, D_LRU), 0.01),
        'lru_w_i': nrm((DEPTH, LRU_BLOCKS, LRU_BLOCK, LRU_BLOCK), LRU_BLOCK ** -0.5),
        'lru_b_i': nrm((DEPTH, D_LRU), 0.01),
        'lru_lambda': lru_lambda,
        'ssm_lambda_re': -0.5 * (1.0 + nrm((DEPTH, SSM_GROUPS, SSM_STATE), 0.02)),
        'ssm_lambda_im': math.pi * n_idx + nrm((DEPTH, SSM_GROUPS, SSM_STATE), 0.01),
        'ssm_log_dt': jax.random.uniform(next(ks), (DEPTH, SSM_GROUPS), jnp.float32,
                                         math.log(0.001), math.log(0.1)),
        'ssm_b_re': nrm((DEPTH, SSM_GROUPS, SSM_STATE, SSM_GROUP), (2.0 * SSM_GROUP) ** -0.5),
        'ssm_b_im': nrm((DEPTH, SSM_GROUPS, SSM_STATE, SSM_GROUP), (2.0 * SSM_GROUP) ** -0.5),
        'ssm_c_re': nrm((DEPTH, SSM_GROUPS, SSM_GROUP, SSM_STATE), (2.0 * SSM_STATE) ** -0.5),
        'ssm_c_im': nrm((DEPTH, SSM_GROUPS, SSM_GROUP, SSM_STATE), (2.0 * SSM_STATE) ** -0.5),
        'ssm_d': nrm((DEPTH, D_SSM), 1.0),
        'w_glu': nrm((DEPTH, D_SSM, 2 * D_MODEL), D_SSM ** -0.5),
        'b_glu': nrm((DEPTH, 2 * D_MODEL), 0.01),
        'w_out': nrm((DEPTH, D_MODEL, D_MODEL), BETA * D_MODEL ** -0.5),
        'b_out': nrm((DEPTH, D_MODEL), 0.01),
        'ln1_g': 1.0 + nrm((DEPTH, D_MODEL), 0.01),
        'ln1_b': nrm((DEPTH, D_MODEL), 0.01),
        'w_up': nrm((DEPTH, D_MODEL, 2 * D_FF), D_MODEL ** -0.5),
        'b_up': nrm((DEPTH, 2 * D_FF), 0.01),
        'ffn_conv_w': nrm((DEPTH, FFN_CONV, D_FF), FFN_CONV ** -0.5),
        'ffn_conv_b': nrm((DEPTH, D_FF), 0.01),
        'w_down': nrm((DEPTH, D_FF, D_MODEL), BETA * D_FF ** -0.5),
        'b_down': nrm((DEPTH, D_MODEL), 0.01),
        'ln2_g': 1.0 + nrm((DEPTH, D_MODEL), 0.01),
        'ln2_b': nrm((DEPTH, D_MODEL), 0.01),
    }


def reference(x_prompt, x_sample, cache_conv_lru, state_lru, state_ssm_re, state_ssm_im, cache_conv_ffn,
              meta_tokens, emb_ln_g, emb_ln_b, w_in, b_in, conv_lru_w, conv_lru_b,
              lru_w_r, lru_b_r, lru_w_i, lru_b_i, lru_lambda,
              ssm_lambda_re, ssm_lambda_im, ssm_log_dt, ssm_b_re, ssm_b_im, ssm_c_re, ssm_c_im, ssm_d,
              w_glu, b_glu, w_out, b_out, ln1_g, ln1_b,
              w_up, b_up, ffn_conv_w, ffn_conv_b, w_down, b_down, ln2_g, ln2_b):
    nbp = x_prompt.shape[0]
    meta = jnp.broadcast_to(meta_tokens[None].astype(x_prompt.dtype), (nbp, N_META, D_MODEL))
    xp = layer_norm(jnp.concatenate([meta, x_prompt], axis=1), emb_ln_g, emb_ln_b)
    xs = layer_norm(x_sample, emb_ln_g, emb_ln_b)
    f32 = jnp.float32
    p_new = ([], [], [], [], [])
    s_new = ([], [], [], [], [])
    for l in range(DEPTH):
        lw = (w_in[l], b_in[l], conv_lru_w[l], conv_lru_b[l], lru_w_r[l], lru_b_r[l], lru_w_i[l], lru_b_i[l],
              lru_lambda[l], ssm_lambda_re[l], ssm_lambda_im[l], ssm_log_dt[l], ssm_b_re[l], ssm_b_im[l],
              ssm_c_re[l], ssm_c_im[l], ssm_d[l], w_glu[l], b_glu[l], w_out[l], b_out[l], ln1_g[l], ln1_b[l],
              w_up[l], b_up[l], ffn_conv_w[l], ffn_conv_b[l], w_down[l], b_down[l], ln2_g[l], ln2_b[l])
        xp, st_p = hybrid_layer(xp,
                                jnp.zeros((nbp, LRU_CONV - 1, D_LRU), xp.dtype),
                                jnp.zeros((nbp, D_LRU), f32),
                                jnp.zeros((nbp, SSM_GROUPS, SSM_STATE), f32),
                                jnp.zeros((nbp, SSM_GROUPS, SSM_STATE), f32),
                                jnp.zeros((nbp, FFN_CONV - 1, D_FF), xp.dtype),
                                *lw)
        xs, st_s = hybrid_layer(xs, cache_conv_lru[l], state_lru[l], state_ssm_re[l], state_ssm_im[l],
                                cache_conv_ffn[l], *lw)
        for lst, val in zip(p_new, st_p):
            lst.append(val)
        for lst, val in zip(s_new, st_s):
            lst.append(val)
    y_prompt = xp[:, N_META:]
    y_sample = xs
    return (y_prompt, y_sample,
            jnp.stack(p_new[0]), jnp.stack(p_new[1]), jnp.stack(p_new[2]), jnp.stack(p_new[3]),
            jnp.stack(p_new[4]),
            jnp.stack(s_new[0]), jnp.stack(s_new[1]), jnp.stack(s_new[2]), jnp.stack(s_new[3]),
            jnp.stack(s_new[4]))
```

```python
import functools
import math

import jax
import jax.numpy as jnp
from jax import lax
from jax.experimental import pallas as pl
from jax.experimental.pallas import tpu as pltpu

D_MODEL = 1024
N_META = 16
D_LRU = D_MODEL
LRU_BLOCKS = 8
LRU_BLOCK = D_LRU // LRU_BLOCKS
LRU_CONV = 4
LRU_C = 8.0
D_SSM = D_MODEL // 2
SSM_GROUP = 16
SSM_GROUPS = D_SSM // SSM_GROUP
SSM_STATE = 64
N_STATE = SSM_GROUPS * SSM_STATE
D_FF = 3 * D_MODEL
FFN_CONV = 3
D_IN = D_LRU + D_SSM + 2 * D_MODEL
DEPTH = 1
ALPHA = (2.0 * DEPTH) ** 0.25
LN_EPS = 1e-5
GELU_C = math.sqrt(2.0 / math.pi)

V7X_VMEM_BYTES = 64 * 1024 * 1024
VMEM_LIMIT_BYTES = V7X_VMEM_BYTES - 8 * 1024 * 1024
SUBLANES = 8
ROW_BLOCK = 64
SSM_SLABS = 2
SSM_LANE_BLOCK = 512
PROMPT_STEPS_PER_CHUNK = 64

BF16 = jnp.bfloat16
F32 = jnp.float32


def _layer_norm(x, g, b):
    mu = jnp.mean(x, axis=-1, keepdims=True)
    xc = x - mu
    var = jnp.mean(xc * xc, axis=-1, keepdims=True)
    return xc * lax.rsqrt(var + LN_EPS) * g + b


def _gelu(x):
    return 0.5 * x * (1.0 + jnp.tanh(GELU_C * (x + 0.044715 * (x * x * x))))


def _sigmoid(x):
    return jax.nn.sigmoid(x)


def _softplus(x):
    return jnp.maximum(x, 0.0) + jnp.log1p(jnp.exp(-jnp.abs(x)))


def _for_row_blocks(n_rows, fn):
    rb = min(ROW_BLOCK, n_rows)
    assert n_rows % rb == 0
    n = n_rows // rb
    if n == 1:
        fn(0, rb)
        return

    def body(i, carry):
        fn(pl.multiple_of(i * rb, rb), rb)
        return carry

    lax.fori_loop(0, n, body, 0)


def _ssm_prep_kernel(lre_ref, lim_ref, logdt_ref, bre_ref, bim_ref,
                     abre_ref, abim_ref, bbre_ref, bbim_ref):
    lre = lre_ref[...]
    lim = lim_ref[...]
    dt = jnp.exp(logdt_ref[...])
    mag = jnp.exp(lre * dt)
    ab_re = mag * jnp.cos(lim * dt)
    ab_im = mag * jnp.sin(lim * dt)
    den = lre * lre + lim * lim
    nr = ab_re - 1.0
    ni = ab_im
    f_re = (nr * lre + ni * lim) / den
    f_im = (ni * lre - nr * lim) / den
    br = bre_ref[...]
    bi = bim_ref[...]
    abre_ref[...] = ab_re
    abim_ref[...] = ab_im
    bbre_ref[...] = f_re * br - f_im * bi
    bbim_ref[...] = f_re * bi + f_im * br


def _ssm_prep(lam_re, lam_im, log_dt, b_re, b_im):
    row = lambda a: a.reshape(1, N_STATE)
    b_t = lambda a: a.transpose(2, 0, 1).reshape(SSM_GROUP, N_STATE)
    logdt = jnp.repeat(log_dt, SSM_STATE).reshape(1, N_STATE)
    out_shape = (jax.ShapeDtypeStruct((1, N_STATE), F32),) * 2 + (
        jax.ShapeDtypeStruct((SSM_GROUP, N_STATE), F32),) * 2
    return pl.pallas_call(_ssm_prep_kernel, out_shape=out_shape, name="ssm_prep")(
        row(lam_re), row(lam_im), logdt, b_t(b_re), b_t(b_im))


def _mixer_kernel(nb, tc,
                  x_ref, conv0_ref, h0_ref, sre0_ref, sim0_ref,
                  embg_ref, embb_ref, win_ref, bin_ref, convw_ref, convb_ref,
                  wri_ref, br_ref, bi_ref, lam_ref,
                  abre_ref, abim_ref, wb_ref, wcre_ref, wcim_ref, ssmd_ref,
                  wglu_ref, bglu_ref, wout_ref, bout_ref, ln1g_ref, ln1b_ref,
                  o_ref, conv_ref, h_ref, sre_ref, sim_ref,
                  zb_ref, xa_ref, us_ref, ga_ref, gb_ref, t1_ref, cb_ref, p_ref, q_ref,
                  y_ref, glb_ref):
    m = nb * tc
    hist = (LRU_CONV - 1) * nb
    step = pl.program_id(0)

    @pl.when(step == 0)
    def _():
        xa_ref[0:hist, :] = conv0_ref[...]
        h_ref[...] = h0_ref[...]
        sre_ref[...] = sre0_ref[...]
        sim_ref[...] = sim0_ref[...]

    def ln_in(off, rb):
        z = _layer_norm(x_ref[pl.ds(off, rb), :], embg_ref[...], embb_ref[...])
        o_ref[pl.ds(off, rb), :] = z
        zb_ref[pl.ds(off, rb), :] = z.astype(BF16)

    _for_row_blocks(m, ln_in)

    def proj(lo, hi):
        return (jnp.dot(zb_ref[...], win_ref[:, lo:hi], preferred_element_type=F32)
                + bin_ref[:, lo:hi])

    c0, c1, c2 = D_LRU, D_LRU + D_SSM, D_LRU + D_SSM + D_MODEL
    xa_ref[hist:hist + m, :] = proj(0, c0)
    us_ref[...] = proj(c0, c1)
    ga_ref[...] = proj(c1, c2)
    gb_ref[...] = proj(c2, D_IN)

    kb = D_SSM // SSM_SLABS
    nbk = N_STATE // SSM_SLABS
    for s in range(SSM_SLABS):
        bu = jnp.dot(us_ref[:, s * kb:(s + 1) * kb].astype(BF16), wb_ref[s],
                     preferred_element_type=F32)
        p_ref[:, s * nbk:(s + 1) * nbk] = bu[:, :nbk]
        q_ref[:, s * nbk:(s + 1) * nbk] = bu[:, nbk:]

    if tc == 1:
        def ssm_step(off, rb):
            rows = pl.ds(off, rb)
            ar, ai = abre_ref[...], abim_ref[...]
            hr, hi = sre0_ref[rows, :], sim0_ref[rows, :]
            nr = ar * hr - ai * hi + p_ref[rows, :]
            ni = ar * hi + ai * hr + q_ref[rows, :]
            p_ref[rows, :] = nr
            q_ref[rows, :] = ni
            sre_ref[rows, :] = nr
            sim_ref[rows, :] = ni

        _for_row_blocks(m, lambda off, rb: [ssm_step(off + k, SUBLANES)
                                            for k in range(0, rb, SUBLANES)])
    else:
        assert nb == SUBLANES
        for lb in range(N_STATE // SSM_LANE_BLOCK):
            lanes = slice(lb * SSM_LANE_BLOCK, (lb + 1) * SSM_LANE_BLOCK)
            ar = jnp.broadcast_to(abre_ref[:, lanes], (nb, SSM_LANE_BLOCK))
            ai = jnp.broadcast_to(abim_ref[:, lanes], (nb, SSM_LANE_BLOCK))

            def ssm_step(t, carry, lanes=lanes, ar=ar, ai=ai):
                hr, hi = carry
                rows = pl.ds(pl.multiple_of(t * nb, nb), nb)
                nr = ar * hr - ai * hi + p_ref[rows, lanes]
                ni = ar * hi + ai * hr + q_ref[rows, lanes]
                p_ref[rows, lanes] = nr
                q_ref[rows, lanes] = ni
                return nr, ni

            hr, hi = lax.fori_loop(0, tc, ssm_step, (sre_ref[:, lanes], sim_ref[:, lanes]),
                                   unroll=8)
            sre_ref[:, lanes] = hr
            sim_ref[:, lanes] = hi

    kc = D_SSM // SSM_SLABS
    for s in range(SSM_SLABS):
        hs = slice(s * nbk, (s + 1) * nbk)
        y_ref[:, s * kc:(s + 1) * kc] = (
            jnp.dot(p_ref[:, hs].astype(BF16), wcre_ref[s], preferred_element_type=F32)
            - jnp.dot(q_ref[:, hs].astype(BF16), wcim_ref[s], preferred_element_type=F32))

    def ssm_out(off, rb):
        rows = pl.ds(off, rb)
        ys = y_ref[rows, :] + ssmd_ref[...] * us_ref[rows, :]
        glb_ref[rows, :] = _gelu(ys).astype(BF16)

    _for_row_blocks(m, ssm_out)

    def conv(off, rb):
        acc = convb_ref[...] + convw_ref[0:1, :] * xa_ref[pl.ds(off, rb), :]
        for j in range(1, LRU_CONV):
            acc = acc + convw_ref[j:j + 1, :] * xa_ref[pl.ds(off + j * nb, rb), :]
        t1_ref[pl.ds(off, rb), :] = acc
        cb_ref[pl.ds(off, rb), :] = acc.astype(BF16)

    _for_row_blocks(m, conv)

    for n in range(LRU_BLOCKS):
        p_ref[:, 2 * n * LRU_BLOCK:(2 * n + 2) * LRU_BLOCK] = jnp.dot(
            cb_ref[:, n * LRU_BLOCK:(n + 1) * LRU_BLOCK], wri_ref[n],
            preferred_element_type=F32)

    sp = _softplus(-lam_ref[...])

    def gates(off, rb):
        rows = pl.ds(off, rb)
        for n in range(LRU_BLOCKS):
            cols = slice(n * LRU_BLOCK, (n + 1) * LRU_BLOCK)
            r = _sigmoid(p_ref[rows, 2 * n * LRU_BLOCK:(2 * n + 1) * LRU_BLOCK] + br_ref[:, cols])
            i = _sigmoid(p_ref[rows, (2 * n + 1) * LRU_BLOCK:(2 * n + 2) * LRU_BLOCK]
                         + bi_ref[:, cols])
            log_a = (-LRU_C * r) * sp[:, cols]
            a = jnp.exp(log_a)
            mult = jnp.sqrt(-jnp.tanh(log_a) * (a * a + 1.0))
            q_ref[rows, cols] = a
            q_ref[rows, D_LRU + n * LRU_BLOCK:D_LRU + (n + 1) * LRU_BLOCK] = (
                mult * (i * t1_ref[rows, cols]))

    _for_row_blocks(m, gates)

    a_cols = slice(0, D_LRU)
    h_cols = slice(D_LRU, 2 * D_LRU)
    if tc == 1:
        def lru_step(off, rb):
            rows = pl.ds(off, rb)
            h = q_ref[rows, a_cols] * h0_ref[rows, :] + q_ref[rows, h_cols]
            q_ref[rows, h_cols] = h
            h_ref[rows, :] = h

        _for_row_blocks(m, lru_step)
    else:
        def lru_step(t, h):
            rows = pl.ds(pl.multiple_of(t * nb, nb), nb)
            h = q_ref[rows, a_cols] * h + q_ref[rows, h_cols]
            q_ref[rows, h_cols] = h
            return h

        h_ref[...] = lax.fori_loop(0, tc, lru_step, h_ref[...], unroll=8)

    p_ref[...] = (jnp.dot(glb_ref[...], wglu_ref[...], preferred_element_type=F32)
                  + bglu_ref[...])

    def merge(off, rb):
        rows = pl.ds(off, rb)
        yb = p_ref[rows, 0:D_MODEL] * _sigmoid(p_ref[rows, D_MODEL:2 * D_MODEL])
        mix = _sigmoid(ga_ref[rows, :]) * q_ref[rows, h_cols] + _sigmoid(gb_ref[rows, :]) * yb
        cb_ref[rows, :] = mix.astype(BF16)

    _for_row_blocks(m, merge)

    t1_ref[...] = (jnp.dot(cb_ref[...], wout_ref[...], preferred_element_type=F32)
                   + bout_ref[...])

    def ln_out(off, rb):
        rows = pl.ds(off, rb)
        o_ref[rows, :] = _layer_norm(ALPHA * o_ref[rows, :] + t1_ref[rows, :],
                                     ln1g_ref[...], ln1b_ref[...])

    _for_row_blocks(m, ln_out)

    tail = xa_ref[m:m + hist, :]
    xa_ref[0:hist, :] = tail
    conv_ref[...] = tail


def _whole(shape):
    return pl.BlockSpec(memory_space=pltpu.VMEM)


def _mixer(x, conv0, h0, sre0, sim0, params, *, nb, tc):
    m_total = x.shape[0]
    m = nb * tc
    assert m_total % m == 0
    hist = (LRU_CONV - 1) * nb
    row_block = pl.BlockSpec((m, D_MODEL), lambda i: (i, 0))
    in_specs = [row_block] + [_whole(None)] * (4 + len(params))
    out_shape = (jax.ShapeDtypeStruct((m_total, D_MODEL), F32),
                 jax.ShapeDtypeStruct((hist, D_LRU), F32),
                 jax.ShapeDtypeStruct((nb, D_LRU), F32),
                 jax.ShapeDtypeStruct((nb, N_STATE), F32),
                 jax.ShapeDtypeStruct((nb, N_STATE), F32))
    out_specs = (row_block,
                 pl.BlockSpec((hist, D_LRU), lambda i: (0, 0)),
                 pl.BlockSpec((nb, D_LRU), lambda i: (0, 0)),
                 pl.BlockSpec((nb, N_STATE), lambda i: (0, 0)),
                 pl.BlockSpec((nb, N_STATE), lambda i: (0, 0)))
    scratch = [
        pltpu.VMEM((m, D_MODEL), BF16),
        pltpu.VMEM((hist + m, D_LRU), F32),
        pltpu.VMEM((m, D_SSM), F32),
        pltpu.VMEM((m, D_MODEL), F32),
        pltpu.VMEM((m, D_MODEL), F32),
        pltpu.VMEM((m, D_MODEL), F32),
        pltpu.VMEM((m, D_MODEL), BF16),
        pltpu.VMEM((m, N_STATE), F32),
        pltpu.VMEM((m, N_STATE), F32),
        pltpu.VMEM((m, D_SSM), F32),
        pltpu.VMEM((m, D_SSM), BF16),
    ]
    return pl.pallas_call(
        functools.partial(_mixer_kernel, nb, tc),
        grid=(m_total // m,),
        in_specs=in_specs,
        out_specs=out_specs,
        out_shape=out_shape,
        scratch_shapes=scratch,
        compiler_params=pltpu.CompilerParams(
            dimension_semantics=("arbitrary",), vmem_limit_bytes=VMEM_LIMIT_BYTES),
        name=f"mixer_nb{nb}_tc{tc}",
    )(x, conv0, h0, sre0, sim0, *params)


def _ffn_kernel(nb, tc,
                x_ref, g0_ref, wup_ref, bup_ref, convw_ref, convb_ref, wdown_ref, bdown_ref,
                ln2g_ref, ln2b_ref,
                o_ref, gnew_ref,
                xb_ref, g_ref, v_ref, hb_ref, t_ref):
    m = nb * tc
    hist = (FFN_CONV - 1) * nb
    step = pl.program_id(0)

    @pl.when(step == 0)
    def _():
        g_ref[0:hist, :] = g0_ref[...]

    def cast_in(off, rb):
        xb_ref[pl.ds(off, rb), :] = x_ref[pl.ds(off, rb), :].astype(BF16)

    _for_row_blocks(m, cast_in)

    for s in range(D_FF // D_MODEL):
        cols = slice(s * D_MODEL, (s + 1) * D_MODEL)
        vcols = slice(D_FF + s * D_MODEL, D_FF + (s + 1) * D_MODEL)
        g_ref[hist:hist + m, cols] = (
            jnp.dot(xb_ref[...], wup_ref[:, cols], preferred_element_type=F32)
            + bup_ref[:, cols])
        v_ref[:, cols] = (
            jnp.dot(xb_ref[...], wup_ref[:, vcols], preferred_element_type=F32)
            + bup_ref[:, vcols])

    def act(off, rb):
        for s in range(D_FF // D_MODEL):
            cols = slice(s * D_MODEL, (s + 1) * D_MODEL)
            acc = convb_ref[:, cols] + convw_ref[0:1, cols] * g_ref[pl.ds(off, rb), cols]
            for j in range(1, FFN_CONV):
                acc = acc + convw_ref[j:j + 1, cols] * g_ref[pl.ds(off + j * nb, rb), cols]
            hb_ref[pl.ds(off, rb), cols] = (_gelu(acc) * v_ref[pl.ds(off, rb), cols]).astype(BF16)

    _for_row_blocks(m, act)

    t_ref[...] = (jnp.dot(hb_ref[...], wdown_ref[...], preferred_element_type=F32)
                  + bdown_ref[...])

    def ln_out(off, rb):
        rows = pl.ds(off, rb)
        o_ref[rows, :] = _layer_norm(ALPHA * x_ref[rows, :] + t_ref[rows, :],
                                     ln2g_ref[...], ln2b_ref[...])

    _for_row_blocks(m, ln_out)

    tail = g_ref[m:m + hist, :]
    g_ref[0:hist, :] = tail
    gnew_ref[...] = tail


def _ffn(x, g0, params, *, nb, tc):
    m_total = x.shape[0]
    m = nb * tc
    assert m_total % m == 0
    hist = (FFN_CONV - 1) * nb
    row_block = pl.BlockSpec((m, D_MODEL), lambda i: (i, 0))
    in_specs = [row_block] + [_whole(None)] * (1 + len(params))
    out_shape = (jax.ShapeDtypeStruct((m_total, D_MODEL), F32),
                 jax.ShapeDtypeStruct((hist, D_FF), F32))
    out_specs = (row_block, pl.BlockSpec((hist, D_FF), lambda i: (0, 0)))
    scratch = [
        pltpu.VMEM((m, D_MODEL), BF16),
        pltpu.VMEM((hist + m, D_FF), F32),
        pltpu.VMEM((m, D_FF), F32),
        pltpu.VMEM((m, D_FF), BF16),
        pltpu.VMEM((m, D_MODEL), F32),
    ]
    return pl.pallas_call(
        functools.partial(_ffn_kernel, nb, tc),
        grid=(m_total // m,),
        in_specs=in_specs,
        out_specs=out_specs,
        out_shape=out_shape,
        scratch_shapes=scratch,
        compiler_params=pltpu.CompilerParams(
            dimension_semantics=("arbitrary",), vmem_limit_bytes=VMEM_LIMIT_BYTES),
        name=f"ffn_nb{nb}_tc{tc}",
    )(x, g0, *params)


def _block_diag_mask(rows_per_group, cols_per_group):
    r = jnp.arange(SSM_GROUPS * rows_per_group)[:, None] // rows_per_group
    c = jnp.arange(SSM_GROUPS * cols_per_group)[None, :] // cols_per_group
    return (r == c).astype(F32)


def _ssm_matrices(bb_re_t, bb_im_t, c_re, c_im):
    mask_b = _block_diag_mask(SSM_GROUP, SSM_STATE)
    wb_re = jnp.tile(bb_re_t, (SSM_GROUPS, 1)) * mask_b
    wb_im = jnp.tile(bb_im_t, (SSM_GROUPS, 1)) * mask_b
    kb = D_SSM // SSM_SLABS
    nbk = N_STATE // SSM_SLABS
    wb = jnp.stack([
        jnp.concatenate([wb_re[s * kb:(s + 1) * kb, s * nbk:(s + 1) * nbk],
                         wb_im[s * kb:(s + 1) * kb, s * nbk:(s + 1) * nbk]], axis=1)
        for s in range(SSM_SLABS)]).astype(BF16)

    mask_c = _block_diag_mask(SSM_STATE, SSM_GROUP)

    def c_dense(c):
        ct = c.transpose(0, 2, 1).reshape(N_STATE, SSM_GROUP)
        return jnp.tile(ct, (1, SSM_GROUPS)) * mask_c

    def c_slabs(c):
        cd = c_dense(c)
        return jnp.stack([cd[s * nbk:(s + 1) * nbk, s * kb:(s + 1) * kb]
                          for s in range(SSM_SLABS)]).astype(BF16)

    return wb, c_slabs(c_re), c_slabs(c_im)


def _to_time_major(state):
    nb, k, c = state.shape
    return state.transpose(1, 0, 2).reshape(k * nb, c)


def _from_time_major(state, nb):
    k = state.shape[0] // nb
    return state.reshape(k, nb, state.shape[1]).transpose(1, 0, 2)


def kernel(x_prompt, x_sample, cache_conv_lru, state_lru, state_ssm_re, state_ssm_im, cache_conv_ffn,
           meta_tokens, emb_ln_g, emb_ln_b, w_in, b_in, conv_lru_w, conv_lru_b,
           lru_w_r, lru_b_r, lru_w_i, lru_b_i, lru_lambda,
           ssm_lambda_re, ssm_lambda_im, ssm_log_dt, ssm_b_re, ssm_b_im, ssm_c_re, ssm_c_im, ssm_d,
           w_glu, b_glu, w_out, b_out, ln1_g, ln1_b,
           w_up, b_up, ffn_conv_w, ffn_conv_b, w_down, b_down, ln2_g, ln2_b):
    assert w_in.shape[0] == DEPTH
    nbp, seq, _ = x_prompt.shape
    nbs = x_sample.shape[0]
    row = lambda a: a.reshape(1, -1).astype(F32)

    ab_re, ab_im, bb_re_t, bb_im_t = _ssm_prep(ssm_lambda_re[0], ssm_lambda_im[0], ssm_log_dt[0],
                                                ssm_b_re[0], ssm_b_im[0])
    wb, wc_re, wc_im = _ssm_matrices(bb_re_t, bb_im_t, ssm_c_re[0], ssm_c_im[0])
    wri = jnp.concatenate([lru_w_r[0], lru_w_i[0]], axis=-1).astype(BF16)

    mixer_params = (row(emb_ln_g), row(emb_ln_b), w_in[0].astype(BF16), row(b_in[0]),
                    conv_lru_w[0], row(conv_lru_b[0]),
                    wri, row(lru_b_r[0]), row(lru_b_i[0]), row(lru_lambda[0]),
                    ab_re, ab_im, wb, wc_re, wc_im, row(ssm_d[0]),
                    w_glu[0].astype(BF16), row(b_glu[0]), w_out[0].astype(BF16), row(b_out[0]),
                    row(ln1_g[0]), row(ln1_b[0]))
    ffn_params = (w_up[0].astype(BF16), row(b_up[0]), ffn_conv_w[0], row(ffn_conv_b[0]),
                  w_down[0].astype(BF16), row(b_down[0]), row(ln2_g[0]), row(ln2_b[0]))

    def layer(x, conv0, h0, sre0, sim0, g0, nb, tc):
        x1, conv, h, sre, sim = _mixer(x, conv0, h0, sre0, sim0, mixer_params, nb=nb, tc=tc)
        x2, g = _ffn(x1, g0, ffn_params, nb=nb, tc=tc)
        return x2, (conv, h, sre, sim, g)

    zeros = lambda r, c: jnp.zeros((r, c), F32)
    meta = jnp.broadcast_to(meta_tokens[:, None, :].astype(F32), (N_META, nbp, D_MODEL))
    _, st = layer(meta.reshape(N_META * nbp, D_MODEL),
                  zeros((LRU_CONV - 1) * nbp, D_LRU), zeros(nbp, D_LRU),
                  zeros(nbp, N_STATE), zeros(nbp, N_STATE), zeros((FFN_CONV - 1) * nbp, D_FF),
                  nbp, N_META)
    x_tm = x_prompt.transpose(1, 0, 2).reshape(seq * nbp, D_MODEL)
    yp, st_p = layer(x_tm, *st, nbp, PROMPT_STEPS_PER_CHUNK)
    y_prompt = yp.reshape(seq, nbp, D_MODEL).transpose(1, 0, 2)

    ys, st_s = layer(x_sample.reshape(nbs, D_MODEL),
                     _to_time_major(cache_conv_lru[0]), state_lru[0],
                     state_ssm_re[0].reshape(nbs, N_STATE), state_ssm_im[0].reshape(nbs, N_STATE),
                     _to_time_major(cache_conv_ffn[0]), nbs, 1)
    y_sample = ys.reshape(nbs, 1, D_MODEL)

    def states(st, nb):
        conv, h, sre, sim, g = st
        return (_from_time_major(conv, nb)[None], h[None],
                sre.reshape(1, nb, SSM_GROUPS, SSM_STATE), sim.reshape(1, nb, SSM_GROUPS, SSM_STATE),
                _from_time_major(g, nb)[None])

    return (y_prompt, y_sample) + states(st_p, nbp) + states(st_s, nbs)
```

```python
import functools
import math

import jax
import jax.numpy as jnp
from jax import lax
from jax.experimental import pallas as pl
from jax.experimental.pallas import tpu as pltpu

D_MODEL = 1024
N_META = 16
D_LRU = D_MODEL
LRU_BLOCKS = 8
LRU_BLOCK = D_LRU // LRU_BLOCKS
LRU_CONV = 4
LRU_C = 8.0
D_SSM = D_MODEL // 2
SSM_GROUP = 16
SSM_GROUPS = D_SSM // SSM_GROUP
SSM_STATE = 64
N_STATE = SSM_GROUPS * SSM_STATE
D_FF = 3 * D_MODEL
FFN_CONV = 3
D_IN = D_LRU + D_SSM + 2 * D_MODEL
DEPTH = 1
ALPHA = (2.0 * DEPTH) ** 0.25
LN_EPS = 1e-5
GELU_C = math.sqrt(2.0 / math.pi)

V7X_VMEM_BYTES = 64 * 1024 * 1024
VMEM_LIMIT_BYTES = V7X_VMEM_BYTES - 8 * 1024 * 1024
SUBLANES = 8
ROW_BLOCK = 64
SSM_SLABS = 2
SSM_LANE_BLOCK = 512
PROMPT_STEPS_PER_CHUNK = 64
FFN_SLAB = 1024

BF16 = jnp.bfloat16
F32 = jnp.float32


def _layer_norm(x, g, b):
    mu = jnp.mean(x, axis=-1, keepdims=True)
    xc = x - mu
    var = jnp.mean(xc * xc, axis=-1, keepdims=True)
    return xc * lax.rsqrt(var + LN_EPS) * g + b


def _gelu(x):
    return 0.5 * x * (1.0 + jnp.tanh(GELU_C * (x + 0.044715 * (x * x * x))))


def _sigmoid(x):
    return jax.nn.sigmoid(x)


def _softplus(x):
    return jnp.maximum(x, 0.0) + jnp.log1p(jnp.exp(-jnp.abs(x)))


def _for_row_blocks(n_rows, fn):
    rb = min(ROW_BLOCK, n_rows)
    assert n_rows % rb == 0
    for off in range(0, n_rows, rb):
        fn(off, rb)


def _ssm_prep_kernel(lre_ref, lim_ref, logdt_ref, bre_ref, bim_ref,
                     abre_ref, abim_ref, bbre_ref, bbim_ref):
    lre = lre_ref[...]
    lim = lim_ref[...]
    dt = jnp.exp(logdt_ref[...])
    mag = jnp.exp(lre * dt)
    ab_re = mag * jnp.cos(lim * dt)
    ab_im = mag * jnp.sin(lim * dt)
    den = lre * lre + lim * lim
    nr = ab_re - 1.0
    ni = ab_im
    f_re = (nr * lre + ni * lim) / den
    f_im = (ni * lre - nr * lim) / den
    br = bre_ref[...]
    bi = bim_ref[...]
    abre_ref[...] = ab_re
    abim_ref[...] = ab_im
    bbre_ref[...] = f_re * br - f_im * bi
    bbim_ref[...] = f_re * bi + f_im * br


def _ssm_prep(lam_re, lam_im, log_dt, b_re, b_im):
    row = lambda a: a.reshape(1, N_STATE)
    b_t = lambda a: a.transpose(2, 0, 1).reshape(SSM_GROUP, N_STATE)
    logdt = jnp.repeat(log_dt, SSM_STATE).reshape(1, N_STATE)
    out_shape = (jax.ShapeDtypeStruct((1, N_STATE), F32),) * 2 + (
        jax.ShapeDtypeStruct((SSM_GROUP, N_STATE), F32),) * 2
    return pl.pallas_call(_ssm_prep_kernel, out_shape=out_shape, name="ssm_prep")(
        row(lam_re), row(lam_im), logdt, b_t(b_re), b_t(b_im))


def _mixer_kernel(nb, tc,
                  x_ref, conv0_ref, h0_ref, sre0_ref, sim0_ref,
                  embg_ref, embb_ref, win_ref, bin_ref, convw_ref, convb_ref,
                  wri_ref, br_ref, bi_ref, lam_ref,
                  abre_ref, abim_ref, wb_ref, wcre_ref, wcim_ref, ssmd_ref,
                  wglu_ref, bglu_ref, wout_ref, bout_ref, ln1g_ref, ln1b_ref,
                  o_ref, conv_ref, h_ref, sre_ref, sim_ref,
                  zb_ref, xa_ref, us_ref, ga_ref, gb_ref, t1_ref, cb_ref, p_ref, q_ref,
                  y_ref, glb_ref):
    m = nb * tc
    hist = (LRU_CONV - 1) * nb
    step = pl.program_id(0)

    @pl.when(step == 0)
    def _():
        xa_ref[0:hist, :] = conv0_ref[...]
        h_ref[...] = h0_ref[...]
        sre_ref[...] = sre0_ref[...]
        sim_ref[...] = sim0_ref[...]

    def ln_in(off, rb):
        z = _layer_norm(x_ref[pl.ds(off, rb), :], embg_ref[...], embb_ref[...])
        o_ref[pl.ds(off, rb), :] = z
        zb_ref[pl.ds(off, rb), :] = z.astype(BF16)

    _for_row_blocks(m, ln_in)

    def proj(lo, hi):
        return (jnp.dot(zb_ref[...], win_ref[:, lo:hi], preferred_element_type=F32)
                + bin_ref[:, lo:hi])

    c0, c1, c2 = D_LRU, D_LRU + D_SSM, D_LRU + D_SSM + D_MODEL
    xa_ref[hist:hist + m, :] = proj(0, c0)
    us_ref[...] = proj(c0, c1)
    ga_ref[...] = proj(c1, c2)
    gb_ref[...] = proj(c2, D_IN)

    kb = D_SSM // SSM_SLABS
    nbk = N_STATE // SSM_SLABS
    for s in range(SSM_SLABS):
        bu = jnp.dot(us_ref[:, s * kb:(s + 1) * kb].astype(BF16), wb_ref[s],
                     preferred_element_type=F32)
        p_ref[:, s * nbk:(s + 1) * nbk] = bu[:, :nbk]
        q_ref[:, s * nbk:(s + 1) * nbk] = bu[:, nbk:]

    if tc == 1:
        def ssm_step(off, rb):
            rows = pl.ds(off, rb)
            ar, ai = abre_ref[...], abim_ref[...]
            hr, hi = sre0_ref[rows, :], sim0_ref[rows, :]
            nr = ar * hr - ai * hi + p_ref[rows, :]
            ni = ar * hi + ai * hr + q_ref[rows, :]
            p_ref[rows, :] = nr
            q_ref[rows, :] = ni
            sre_ref[rows, :] = nr
            sim_ref[rows, :] = ni

        _for_row_blocks(m, lambda off, rb: [ssm_step(off + k, SUBLANES)
                                            for k in range(0, rb, SUBLANES)])
    else:
        assert nb == SUBLANES
        for lb in range(N_STATE // SSM_LANE_BLOCK):
            lanes = slice(lb * SSM_LANE_BLOCK, (lb + 1) * SSM_LANE_BLOCK)
            ar = jnp.broadcast_to(abre_ref[:, lanes], (nb, SSM_LANE_BLOCK))
            ai = jnp.broadcast_to(abim_ref[:, lanes], (nb, SSM_LANE_BLOCK))
            hr, hi = sre_ref[:, lanes], sim_ref[:, lanes]
            for t in range(tc):
                rows = pl.ds(t * nb, nb)
                hr, hi = (ar * hr - ai * hi + p_ref[rows, lanes],
                          ar * hi + ai * hr + q_ref[rows, lanes])
                p_ref[rows, lanes] = hr
                q_ref[rows, lanes] = hi
            sre_ref[:, lanes] = hr
            sim_ref[:, lanes] = hi

    kc = D_SSM // SSM_SLABS
    for s in range(SSM_SLABS):
        hs = slice(s * nbk, (s + 1) * nbk)
        y_ref[:, s * kc:(s + 1) * kc] = (
            jnp.dot(p_ref[:, hs].astype(BF16), wcre_ref[s], preferred_element_type=F32)
            - jnp.dot(q_ref[:, hs].astype(BF16), wcim_ref[s], preferred_element_type=F32))

    def ssm_out(off, rb):
        rows = pl.ds(off, rb)
        ys = y_ref[rows, :] + ssmd_ref[...] * us_ref[rows, :]
        glb_ref[rows, :] = _gelu(ys).astype(BF16)

    _for_row_blocks(m, ssm_out)

    def conv(off, rb):
        acc = convb_ref[...] + convw_ref[0:1, :] * xa_ref[pl.ds(off, rb), :]
        for j in range(1, LRU_CONV):
            acc = acc + convw_ref[j:j + 1, :] * xa_ref[pl.ds(off + j * nb, rb), :]
        t1_ref[pl.ds(off, rb), :] = acc
        cb_ref[pl.ds(off, rb), :] = acc.astype(BF16)

    _for_row_blocks(m, conv)

    for n in range(LRU_BLOCKS):
        p_ref[:, 2 * n * LRU_BLOCK:(2 * n + 2) * LRU_BLOCK] = jnp.dot(
            cb_ref[:, n * LRU_BLOCK:(n + 1) * LRU_BLOCK], wri_ref[n],
            preferred_element_type=F32)

    sp = _softplus(-lam_ref[...])

    def gates(off, rb):
        rows = pl.ds(off, rb)
        for n in range(LRU_BLOCKS):
            cols = slice(n * LRU_BLOCK, (n + 1) * LRU_BLOCK)
            r = _sigmoid(p_ref[rows, 2 * n * LRU_BLOCK:(2 * n + 1) * LRU_BLOCK] + br_ref[:, cols])
            i = _sigmoid(p_ref[rows, (2 * n + 1) * LRU_BLOCK:(2 * n + 2) * LRU_BLOCK]
                         + bi_ref[:, cols])
            log_a = (-LRU_C * r) * sp[:, cols]
            a = jnp.exp(log_a)
            mult = jnp.sqrt(-jnp.tanh(log_a) * (a * a + 1.0))
            q_ref[rows, cols] = a
            q_ref[rows, D_LRU + n * LRU_BLOCK:D_LRU + (n + 1) * LRU_BLOCK] = (
                mult * (i * t1_ref[rows, cols]))

    _for_row_blocks(m, gates)

    a_cols = slice(0, D_LRU)
    h_cols = slice(D_LRU, 2 * D_LRU)
    if tc == 1:
        def lru_step(off, rb):
            rows = pl.ds(off, rb)
            h = q_ref[rows, a_cols] * h0_ref[rows, :] + q_ref[rows, h_cols]
            q_ref[rows, h_cols] = h
            h_ref[rows, :] = h

        _for_row_blocks(m, lru_step)
    else:
        h = h_ref[...]
        for t in range(tc):
            rows = pl.ds(t * nb, nb)
            h = q_ref[rows, a_cols] * h + q_ref[rows, h_cols]
            q_ref[rows, h_cols] = h
        h_ref[...] = h

    p_ref[...] = (jnp.dot(glb_ref[...], wglu_ref[...], preferred_element_type=F32)
                  + bglu_ref[...])

    def merge(off, rb):
        rows = pl.ds(off, rb)
        yb = p_ref[rows, 0:D_MODEL] * _sigmoid(p_ref[rows, D_MODEL:2 * D_MODEL])
        mix = _sigmoid(ga_ref[rows, :]) * q_ref[rows, h_cols] + _sigmoid(gb_ref[rows, :]) * yb
        cb_ref[rows, :] = mix.astype(BF16)

    _for_row_blocks(m, merge)

    t1_ref[...] = (jnp.dot(cb_ref[...], wout_ref[...], preferred_element_type=F32)
                   + bout_ref[...])

    def ln_out(off, rb):
        rows = pl.ds(off, rb)
        o_ref[rows, :] = _layer_norm(ALPHA * o_ref[rows, :] + t1_ref[rows, :],
                                     ln1g_ref[...], ln1b_ref[...])

    _for_row_blocks(m, ln_out)

    tail = xa_ref[m:m + hist, :]
    xa_ref[0:hist, :] = tail
    conv_ref[...] = tail


def _whole(shape):
    return pl.BlockSpec(memory_space=pltpu.VMEM)


def _mixer(x, conv0, h0, sre0, sim0, params, *, nb, tc):
    m_total = x.shape[0]
    m = nb * tc
    assert m_total % m == 0
    hist = (LRU_CONV - 1) * nb
    row_block = pl.BlockSpec((m, D_MODEL), lambda i: (i, 0))
    in_specs = [row_block] + [_whole(None)] * (4 + len(params))
    out_shape = (jax.ShapeDtypeStruct((m_total, D_MODEL), F32),
                 jax.ShapeDtypeStruct((hist, D_LRU), F32),
                 jax.ShapeDtypeStruct((nb, D_LRU), F32),
                 jax.ShapeDtypeStruct((nb, N_STATE), F32),
                 jax.ShapeDtypeStruct((nb, N_STATE), F32))
    out_specs = (row_block,
                 pl.BlockSpec((hist, D_LRU), lambda i: (0, 0)),
                 pl.BlockSpec((nb, D_LRU), lambda i: (0, 0)),
                 pl.BlockSpec((nb, N_STATE), lambda i: (0, 0)),
                 pl.BlockSpec((nb, N_STATE), lambda i: (0, 0)))
    scratch = [
        pltpu.VMEM((m, D_MODEL), BF16),
        pltpu.VMEM((hist + m, D_LRU), F32),
        pltpu.VMEM((m, D_SSM), F32),
        pltpu.VMEM((m, D_MODEL), F32),
        pltpu.VMEM((m, D_MODEL), F32),
        pltpu.VMEM((m, D_MODEL), F32),
        pltpu.VMEM((m, D_MODEL), BF16),
        pltpu.VMEM((m, N_STATE), F32),
        pltpu.VMEM((m, N_STATE), F32),
        pltpu.VMEM((m, D_SSM), F32),
        pltpu.VMEM((m, D_SSM), BF16),
    ]
    return pl.pallas_call(
        functools.partial(_mixer_kernel, nb, tc),
        grid=(m_total // m,),
        in_specs=in_specs,
        out_specs=out_specs,
        out_shape=out_shape,
        scratch_shapes=scratch,
        compiler_params=pltpu.CompilerParams(
            dimension_semantics=("arbitrary",), vmem_limit_bytes=VMEM_LIMIT_BYTES),
        name=f"mixer_nb{nb}_tc{tc}",
    )(x, conv0, h0, sre0, sim0, *params)


def _ffn_kernel(nb, tc,
                x_ref, g0_ref, wup_ref, bup_ref, convw_ref, convb_ref, wdown_ref, bdown_ref,
                ln2g_ref, ln2b_ref,
                o_ref, gnew_ref,
                xb_ref, g_ref, v_ref, hb_ref, t_ref):
    m = nb * tc
    hist = (FFN_CONV - 1) * nb
    step = pl.program_id(0)
    rb = min(ROW_BLOCK, m)
    row_blocks = [pl.ds(off, rb) for off in range(0, m, rb)]

    @pl.when(step == 0)
    def _():
        g_ref[0:hist, :] = g0_ref[...]

    for rows in row_blocks:
        xb_ref[rows, :] = x_ref[rows, :].astype(BF16)

    for s in range(D_FF // FFN_SLAB):
        cols = slice(s * FFN_SLAB, (s + 1) * FFN_SLAB)
        vcols = slice(D_FF + s * FFN_SLAB, D_FF + (s + 1) * FFN_SLAB)
        g_ref[hist:hist + m, cols] = (
            jnp.dot(xb_ref[...], wup_ref[:, cols], preferred_element_type=F32)
            + bup_ref[:, cols])
        v_ref[:, cols] = (
            jnp.dot(xb_ref[...], wup_ref[:, vcols], preferred_element_type=F32)
            + bup_ref[:, vcols])
        for rows in row_blocks:
            acc = convb_ref[:, cols] + convw_ref[0:1, cols] * g_ref[rows, cols]
            for j in range(1, FFN_CONV):
                shifted = pl.ds(rows.start + j * nb, rb)
                acc = acc + convw_ref[j:j + 1, cols] * g_ref[shifted, cols]
            hb_ref[rows, cols] = (_gelu(acc) * v_ref[rows, cols]).astype(BF16)
        part = jnp.dot(hb_ref[:, cols], wdown_ref[cols, :], preferred_element_type=F32)
        if s == 0:
            t_ref[...] = part + bdown_ref[...]
        else:
            t_ref[...] += part

    for rows in row_blocks:
        o_ref[rows, :] = _layer_norm(ALPHA * x_ref[rows, :] + t_ref[rows, :],
                                     ln2g_ref[...], ln2b_ref[...])

    tail = g_ref[m:m + hist, :]
    g_ref[0:hist, :] = tail
    gnew_ref[...] = tail


def _ffn(x, g0, params, *, nb, tc):
    m_total = x.shape[0]
    m = nb * tc
    assert m_total % m == 0
    hist = (FFN_CONV - 1) * nb
    row_block = pl.BlockSpec((m, D_MODEL), lambda i: (i, 0))
    in_specs = [row_block] + [_whole(None)] * (1 + len(params))
    out_shape = (jax.ShapeDtypeStruct((m_total, D_MODEL), F32),
                 jax.ShapeDtypeStruct((hist, D_FF), F32))
    out_specs = (row_block, pl.BlockSpec((hist, D_FF), lambda i: (0, 0)))
    scratch = [
        pltpu.VMEM((m, D_MODEL), BF16),
        pltpu.VMEM((hist + m, D_FF), F32),
        pltpu.VMEM((m, D_FF), F32),
        pltpu.VMEM((m, D_FF), BF16),
        pltpu.VMEM((m, D_MODEL), F32),
    ]
    return pl.pallas_call(
        functools.partial(_ffn_kernel, nb, tc),
        grid=(m_total // m,),
        in_specs=in_specs,
        out_specs=out_specs,
        out_shape=out_shape,
        scratch_shapes=scratch,
        compiler_params=pltpu.CompilerParams(
            dimension_semantics=("arbitrary",), vmem_limit_bytes=VMEM_LIMIT_BYTES),
        name=f"ffn_nb{nb}_tc{tc}",
    )(x, g0, *params)


def _block_diag_mask(rows_per_group, cols_per_group):
    r = jnp.arange(SSM_GROUPS * rows_per_group)[:, None] // rows_per_group
    c = jnp.arange(SSM_GROUPS * cols_per_group)[None, :] // cols_per_group
    return (r == c).astype(F32)


def _ssm_matrices(bb_re_t, bb_im_t, c_re, c_im):
    mask_b = _block_diag_mask(SSM_GROUP, SSM_STATE)
    wb_re = jnp.tile(bb_re_t, (SSM_GROUPS, 1)) * mask_b
    wb_im = jnp.tile(bb_im_t, (SSM_GROUPS, 1)) * mask_b
    kb = D_SSM // SSM_SLABS
    nbk = N_STATE // SSM_SLABS
    wb = jnp.stack([
        jnp.concatenate([wb_re[s * kb:(s + 1) * kb, s * nbk:(s + 1) * nbk],
                         wb_im[s * kb:(s + 1) * kb, s * nbk:(s + 1) * nbk]], axis=1)
        for s in range(SSM_SLABS)]).astype(BF16)

    mask_c = _block_diag_mask(SSM_STATE, SSM_GROUP)

    def c_dense(c):
        ct = c.transpose(0, 2, 1).reshape(N_STATE, SSM_GROUP)
        return jnp.tile(ct, (1, SSM_GROUPS)) * mask_c

    def c_slabs(c):
        cd = c_dense(c)
        return jnp.stack([cd[s * nbk:(s + 1) * nbk, s * kb:(s + 1) * kb]
                          for s in range(SSM_SLABS)]).astype(BF16)

    return wb, c_slabs(c_re), c_slabs(c_im)


def _to_time_major(state):
    nb, k, c = state.shape
    return state.transpose(1, 0, 2).reshape(k * nb, c)


def _from_time_major(state, nb):
    k = state.shape[0] // nb
    return state.reshape(k, nb, state.shape[1]).transpose(1, 0, 2)


def kernel(x_prompt, x_sample, cache_conv_lru, state_lru, state_ssm_re, state_ssm_im, cache_conv_ffn,
           meta_tokens, emb_ln_g, emb_ln_b, w_in, b_in, conv_lru_w, conv_lru_b,
           lru_w_r, lru_b_r, lru_w_i, lru_b_i, lru_lambda,
           ssm_lambda_re, ssm_lambda_im, ssm_log_dt, ssm_b_re, ssm_b_im, ssm_c_re, ssm_c_im, ssm_d,
           w_glu, b_glu, w_out, b_out, ln1_g, ln1_b,
           w_up, b_up, ffn_conv_w, ffn_conv_b, w_down, b_down, ln2_g, ln2_b):
    assert w_in.shape[0] == DEPTH
    nbp, seq, _ = x_prompt.shape
    nbs = x_sample.shape[0]
    row = lambda a: a.reshape(1, -1).astype(F32)

    ab_re, ab_im, bb_re_t, bb_im_t = _ssm_prep(ssm_lambda_re[0], ssm_lambda_im[0], ssm_log_dt[0],
                                                ssm_b_re[0], ssm_b_im[0])
    wb, wc_re, wc_im = _ssm_matrices(bb_re_t, bb_im_t, ssm_c_re[0], ssm_c_im[0])
    wri = jnp.concatenate([lru_w_r[0], lru_w_i[0]], axis=-1).astype(BF16)

    mixer_params = (row(emb_ln_g), row(emb_ln_b), w_in[0].astype(BF16), row(b_in[0]),
                    conv_lru_w[0], row(conv_lru_b[0]),
                    wri, row(lru_b_r[0]), row(lru_b_i[0]), row(lru_lambda[0]),
                    ab_re, ab_im, wb, wc_re, wc_im, row(ssm_d[0]),
                    w_glu[0].astype(BF16), row(b_glu[0]), w_out[0].astype(BF16), row(b_out[0]),
                    row(ln1_g[0]), row(ln1_b[0]))
    ffn_params = (w_up[0].astype(BF16), row(b_up[0]), ffn_conv_w[0], row(ffn_conv_b[0]),
                  w_down[0].astype(BF16), row(b_down[0]), row(ln2_g[0]), row(ln2_b[0]))

    def layer(x, conv0, h0, sre0, sim0, g0, nb, tc):
        x1, conv, h, sre, sim = _mixer(x, conv0, h0, sre0, sim0, mixer_params, nb=nb, tc=tc)
        x2, g = _ffn(x1, g0, ffn_params, nb=nb, tc=tc)
        return x2, (conv, h, sre, sim, g)

    zeros = lambda r, c: jnp.zeros((r, c), F32)
    meta = jnp.broadcast_to(meta_tokens[:, None, :].astype(F32), (N_META, nbp, D_MODEL))
    _, st = layer(meta.reshape(N_META * nbp, D_MODEL),
                  zeros((LRU_CONV - 1) * nbp, D_LRU), zeros(nbp, D_LRU),
                  zeros(nbp, N_STATE), zeros(nbp, N_STATE), zeros((FFN_CONV - 1) * nbp, D_FF),
                  nbp, N_META)
    x_tm = x_prompt.transpose(1, 0, 2).reshape(seq * nbp, D_MODEL)
    yp, st_p = layer(x_tm, *st, nbp, PROMPT_STEPS_PER_CHUNK)
    y_prompt = yp.reshape(seq, nbp, D_MODEL).transpose(1, 0, 2)

    ys, st_s = layer(x_sample.reshape(nbs, D_MODEL),
                     _to_time_major(cache_conv_lru[0]), state_lru[0],
                     state_ssm_re[0].reshape(nbs, N_STATE), state_ssm_im[0].reshape(nbs, N_STATE),
                     _to_time_major(cache_conv_ffn[0]), nbs, 1)
    y_sample = ys.reshape(nbs, 1, D_MODEL)

    def states(st, nb):
        conv, h, sre, sim, g = st
        return (_from_time_major(conv, nb)[None], h[None],
                sre.reshape(1, nb, SSM_GROUPS, SSM_STATE), sim.reshape(1, nb, SSM_GROUPS, SSM_STATE),
                _from_time_major(g, nb)[None])

    return (y_prompt, y_sample) + states(st_p, nbp) + states(st_s, nbs)
```

```python
import functools
import math

import jax
import jax.numpy as jnp
from jax import lax
from jax.experimental import pallas as pl
from jax.experimental.pallas import tpu as pltpu

D_MODEL = 1024
N_META = 16
D_LRU = D_MODEL
LRU_BLOCKS = 8
LRU_BLOCK = D_LRU // LRU_BLOCKS
LRU_CONV = 4
LRU_C = 8.0
D_SSM = D_MODEL // 2
SSM_GROUP = 16
SSM_GROUPS = D_SSM // SSM_GROUP
SSM_STATE = 64
N_STATE = SSM_GROUPS * SSM_STATE
D_FF = 3 * D_MODEL
FFN_CONV = 3
D_IN = D_LRU + D_SSM + 2 * D_MODEL
DEPTH = 1
ALPHA = (2.0 * DEPTH) ** 0.25
LN_EPS = 1e-5
GELU_C = math.sqrt(2.0 / math.pi)

V7X_VMEM_BYTES = 64 * 1024 * 1024
VMEM_LIMIT_BYTES = V7X_VMEM_BYTES - 8 * 1024 * 1024
SUBLANES = 8
ROW_BLOCK = 64
SSM_SLABS = 2
SSM_LANE_BLOCK = 512
PROMPT_STEPS_PER_CHUNK = 64
FFN_SLAB = 1024

BF16 = jnp.bfloat16
F32 = jnp.float32


def _layer_norm(x, g, b):
    mu = jnp.mean(x, axis=-1, keepdims=True)
    xc = x - mu
    var = jnp.mean(xc * xc, axis=-1, keepdims=True)
    return xc * lax.rsqrt(var + LN_EPS) * g + b


def _gelu(x):
    return 0.5 * x * (1.0 + jnp.tanh(GELU_C * (x + 0.044715 * (x * x * x))))


def _sigmoid(x):
    return jax.nn.sigmoid(x)


def _softplus(x):
    return jnp.maximum(x, 0.0) + jnp.log1p(jnp.exp(-jnp.abs(x)))


def _for_row_blocks(n_rows, fn):
    rb = min(ROW_BLOCK, n_rows)
    assert n_rows % rb == 0
    for off in range(0, n_rows, rb):
        fn(off, rb)


def _ssm_prep_kernel(lre_ref, lim_ref, logdt_ref, bre_ref, bim_ref,
                     abre_ref, abim_ref, bbre_ref, bbim_ref):
    lre = lre_ref[...]
    lim = lim_ref[...]
    dt = jnp.exp(logdt_ref[...])
    mag = jnp.exp(lre * dt)
    ab_re = mag * jnp.cos(lim * dt)
    ab_im = mag * jnp.sin(lim * dt)
    den = lre * lre + lim * lim
    nr = ab_re - 1.0
    ni = ab_im
    f_re = (nr * lre + ni * lim) / den
    f_im = (ni * lre - nr * lim) / den
    br = bre_ref[...]
    bi = bim_ref[...]
    abre_ref[...] = ab_re
    abim_ref[...] = ab_im
    bbre_ref[...] = f_re * br - f_im * bi
    bbim_ref[...] = f_re * bi + f_im * br


def _ssm_prep(lam_re, lam_im, log_dt, b_re, b_im):
    row = lambda a: a.reshape(1, N_STATE)
    b_t = lambda a: a.transpose(2, 0, 1).reshape(SSM_GROUP, N_STATE)
    logdt = jnp.repeat(log_dt, SSM_STATE).reshape(1, N_STATE)
    out_shape = (jax.ShapeDtypeStruct((1, N_STATE), F32),) * 2 + (
        jax.ShapeDtypeStruct((SSM_GROUP, N_STATE), F32),) * 2
    return pl.pallas_call(_ssm_prep_kernel, out_shape=out_shape, name="ssm_prep")(
        row(lam_re), row(lam_im), logdt, b_t(b_re), b_t(b_im))


def _mixer_kernel(nb, tc,
                  x_ref, conv0_ref, h0_ref, sre0_ref, sim0_ref,
                  embg_ref, embb_ref, win_ref, bin_ref, convw_ref, convb_ref,
                  wri_ref, br_ref, bi_ref, lam_ref,
                  abre_ref, abim_ref, wb_ref, wcre_ref, wcim_ref, ssmd_ref,
                  wglu_ref, bglu_ref, wout_ref, bout_ref, ln1g_ref, ln1b_ref,
                  o_ref, conv_ref, h_ref, sre_ref, sim_ref,
                  zb_ref, xa_ref, us_ref, ga_ref, gb_ref, t1_ref, cb_ref, p_ref, q_ref,
                  y_ref, glb_ref):
    m = nb * tc
    hist = (LRU_CONV - 1) * nb
    step = pl.program_id(0)

    @pl.when(step == 0)
    def _():
        xa_ref[0:hist, :] = conv0_ref[...]
        h_ref[...] = h0_ref[...]
        sre_ref[...] = sre0_ref[...]
        sim_ref[...] = sim0_ref[...]

    if len(x_ref.shape) == 3:
        for t in range(tc):
            o_ref[pl.ds(t * nb, nb), :] = x_ref[:, t, :]
        src_ref = o_ref
    else:
        src_ref = x_ref

    def ln_in(off, rb):
        z = _layer_norm(src_ref[pl.ds(off, rb), :], embg_ref[...], embb_ref[...])
        o_ref[pl.ds(off, rb), :] = z
        zb_ref[pl.ds(off, rb), :] = z.astype(BF16)

    _for_row_blocks(m, ln_in)

    def proj(lo, hi):
        return (jnp.dot(zb_ref[...], win_ref[:, lo:hi], preferred_element_type=F32)
                + bin_ref[:, lo:hi])

    c0, c1, c2 = D_LRU, D_LRU + D_SSM, D_LRU + D_SSM + D_MODEL
    xa_ref[hist:hist + m, :] = proj(0, c0)
    us_ref[...] = proj(c0, c1)
    ga_ref[...] = proj(c1, c2)
    gb_ref[...] = proj(c2, D_IN)

    kb = D_SSM // SSM_SLABS
    nbk = N_STATE // SSM_SLABS
    for s in range(SSM_SLABS):
        bu = jnp.dot(us_ref[:, s * kb:(s + 1) * kb].astype(BF16), wb_ref[s],
                     preferred_element_type=F32)
        p_ref[:, s * nbk:(s + 1) * nbk] = bu[:, :nbk]
        q_ref[:, s * nbk:(s + 1) * nbk] = bu[:, nbk:]

    if tc == 1:
        def ssm_step(off, rb):
            rows = pl.ds(off, rb)
            ar, ai = abre_ref[...], abim_ref[...]
            hr, hi = sre0_ref[rows, :], sim0_ref[rows, :]
            nr = ar * hr - ai * hi + p_ref[rows, :]
            ni = ar * hi + ai * hr + q_ref[rows, :]
            p_ref[rows, :] = nr
            q_ref[rows, :] = ni
            sre_ref[rows, :] = nr
            sim_ref[rows, :] = ni

        _for_row_blocks(m, lambda off, rb: [ssm_step(off + k, SUBLANES)
                                            for k in range(0, rb, SUBLANES)])
    else:
        assert nb == SUBLANES
        for lb in range(N_STATE // SSM_LANE_BLOCK):
            lanes = slice(lb * SSM_LANE_BLOCK, (lb + 1) * SSM_LANE_BLOCK)
            ar = jnp.broadcast_to(abre_ref[:, lanes], (nb, SSM_LANE_BLOCK))
            ai = jnp.broadcast_to(abim_ref[:, lanes], (nb, SSM_LANE_BLOCK))
            hr, hi = sre_ref[:, lanes], sim_ref[:, lanes]
            for t in range(tc):
                rows = pl.ds(t * nb, nb)
                hr, hi = (ar * hr - ai * hi + p_ref[rows, lanes],
                          ar * hi + ai * hr + q_ref[rows, lanes])
                p_ref[rows, lanes] = hr
                q_ref[rows, lanes] = hi
            sre_ref[:, lanes] = hr
            sim_ref[:, lanes] = hi

    kc = D_SSM // SSM_SLABS
    for s in range(SSM_SLABS):
        hs = slice(s * nbk, (s + 1) * nbk)
        y_ref[:, s * kc:(s + 1) * kc] = (
            jnp.dot(p_ref[:, hs].astype(BF16), wcre_ref[s], preferred_element_type=F32)
            - jnp.dot(q_ref[:, hs].astype(BF16), wcim_ref[s], preferred_element_type=F32))

    def ssm_out(off, rb):
        rows = pl.ds(off, rb)
        ys = y_ref[rows, :] + ssmd_ref[...] * us_ref[rows, :]
        glb_ref[rows, :] = _gelu(ys).astype(BF16)

    _for_row_blocks(m, ssm_out)

    def conv(off, rb):
        acc = convb_ref[...] + convw_ref[0:1, :] * xa_ref[pl.ds(off, rb), :]
        for j in range(1, LRU_CONV):
            acc = acc + convw_ref[j:j + 1, :] * xa_ref[pl.ds(off + j * nb, rb), :]
        t1_ref[pl.ds(off, rb), :] = acc
        cb_ref[pl.ds(off, rb), :] = acc.astype(BF16)

    _for_row_blocks(m, conv)

    for n in range(LRU_BLOCKS):
        p_ref[:, 2 * n * LRU_BLOCK:(2 * n + 2) * LRU_BLOCK] = jnp.dot(
            cb_ref[:, n * LRU_BLOCK:(n + 1) * LRU_BLOCK], wri_ref[n],
            preferred_element_type=F32)

    sp = _softplus(-lam_ref[...])

    def gates(off, rb):
        rows = pl.ds(off, rb)
        for n in range(LRU_BLOCKS):
            cols = slice(n * LRU_BLOCK, (n + 1) * LRU_BLOCK)
            r = _sigmoid(p_ref[rows, 2 * n * LRU_BLOCK:(2 * n + 1) * LRU_BLOCK] + br_ref[:, cols])
            i = _sigmoid(p_ref[rows, (2 * n + 1) * LRU_BLOCK:(2 * n + 2) * LRU_BLOCK]
                         + bi_ref[:, cols])
            log_a = (-LRU_C * r) * sp[:, cols]
            a = jnp.exp(log_a)
            mult = jnp.sqrt(-jnp.tanh(log_a) * (a * a + 1.0))
            q_ref[rows, cols] = a
            q_ref[rows, D_LRU + n * LRU_BLOCK:D_LRU + (n + 1) * LRU_BLOCK] = (
                mult * (i * t1_ref[rows, cols]))

    _for_row_blocks(m, gates)

    a_cols = slice(0, D_LRU)
    h_cols = slice(D_LRU, 2 * D_LRU)
    if tc == 1:
        def lru_step(off, rb):
            rows = pl.ds(off, rb)
            h = q_ref[rows, a_cols] * h0_ref[rows, :] + q_ref[rows, h_cols]
            q_ref[rows, h_cols] = h
            h_ref[rows, :] = h

        _for_row_blocks(m, lru_step)
    else:
        h = h_ref[...]
        for t in range(tc):
            rows = pl.ds(t * nb, nb)
            h = q_ref[rows, a_cols] * h + q_ref[rows, h_cols]
            q_ref[rows, h_cols] = h
        h_ref[...] = h

    p_ref[...] = (jnp.dot(glb_ref[...], wglu_ref[...], preferred_element_type=F32)
                  + bglu_ref[...])

    def merge(off, rb):
        rows = pl.ds(off, rb)
        yb = p_ref[rows, 0:D_MODEL] * _sigmoid(p_ref[rows, D_MODEL:2 * D_MODEL])
        mix = _sigmoid(ga_ref[rows, :]) * q_ref[rows, h_cols] + _sigmoid(gb_ref[rows, :]) * yb
        cb_ref[rows, :] = mix.astype(BF16)

    _for_row_blocks(m, merge)

    t1_ref[...] = (jnp.dot(cb_ref[...], wout_ref[...], preferred_element_type=F32)
                   + bout_ref[...])

    def ln_out(off, rb):
        rows = pl.ds(off, rb)
        o_ref[rows, :] = _layer_norm(ALPHA * o_ref[rows, :] + t1_ref[rows, :],
                                     ln1g_ref[...], ln1b_ref[...])

    _for_row_blocks(m, ln_out)

    tail = xa_ref[m:m + hist, :]
    xa_ref[0:hist, :] = tail
    conv_ref[...] = tail


def _whole(shape):
    return pl.BlockSpec(memory_space=pltpu.VMEM)


def _mixer(x, conv0, h0, sre0, sim0, params, *, nb, tc):
    m = nb * tc
    row_block = pl.BlockSpec((m, D_MODEL), lambda i: (i, 0))
    if x.ndim == 3:
        assert x.shape[0] == nb and x.shape[1] % tc == 0
        m_total = nb * x.shape[1]
        x_spec = pl.BlockSpec((nb, tc, D_MODEL), lambda i: (0, i, 0))
    else:
        m_total = x.shape[0]
        x_spec = row_block
    assert m_total % m == 0
    hist = (LRU_CONV - 1) * nb
    in_specs = [x_spec] + [_whole(None)] * (4 + len(params))
    out_shape = (jax.ShapeDtypeStruct((m_total, D_MODEL), F32),
                 jax.ShapeDtypeStruct((hist, D_LRU), F32),
                 jax.ShapeDtypeStruct((nb, D_LRU), F32),
                 jax.ShapeDtypeStruct((nb, N_STATE), F32),
                 jax.ShapeDtypeStruct((nb, N_STATE), F32))
    out_specs = (row_block,
                 pl.BlockSpec((hist, D_LRU), lambda i: (0, 0)),
                 pl.BlockSpec((nb, D_LRU), lambda i: (0, 0)),
                 pl.BlockSpec((nb, N_STATE), lambda i: (0, 0)),
                 pl.BlockSpec((nb, N_STATE), lambda i: (0, 0)))
    scratch = [
        pltpu.VMEM((m, D_MODEL), BF16),
        pltpu.VMEM((hist + m, D_LRU), F32),
        pltpu.VMEM((m, D_SSM), F32),
        pltpu.VMEM((m, D_MODEL), F32),
        pltpu.VMEM((m, D_MODEL), F32),
        pltpu.VMEM((m, D_MODEL), F32),
        pltpu.VMEM((m, D_MODEL), BF16),
        pltpu.VMEM((m, N_STATE), F32),
        pltpu.VMEM((m, N_STATE), F32),
        pltpu.VMEM((m, D_SSM), F32),
        pltpu.VMEM((m, D_SSM), BF16),
    ]
    return pl.pallas_call(
        functools.partial(_mixer_kernel, nb, tc),
        grid=(m_total // m,),
        in_specs=in_specs,
        out_specs=out_specs,
        out_shape=out_shape,
        scratch_shapes=scratch,
        compiler_params=pltpu.CompilerParams(
            dimension_semantics=("arbitrary",), vmem_limit_bytes=VMEM_LIMIT_BYTES),
        name=f"mixer_nb{nb}_tc{tc}",
    )(x, conv0, h0, sre0, sim0, *params)


def _ffn_kernel(nb, tc,
                x_ref, g0_ref, wup_ref, bup_ref, convw_ref, convb_ref, wdown_ref, bdown_ref,
                ln2g_ref, ln2b_ref,
                o_ref, gnew_ref,
                xb_ref, g_ref, v_ref, hb_ref, t_ref):
    m = nb * tc
    hist = (FFN_CONV - 1) * nb
    step = pl.program_id(0)
    rb = min(ROW_BLOCK, m)
    row_blocks = [pl.ds(off, rb) for off in range(0, m, rb)]

    @pl.when(step == 0)
    def _():
        g_ref[0:hist, :] = g0_ref[...]

    for rows in row_blocks:
        xb_ref[rows, :] = x_ref[rows, :].astype(BF16)

    for s in range(D_FF // FFN_SLAB):
        cols = slice(s * FFN_SLAB, (s + 1) * FFN_SLAB)
        vcols = slice(D_FF + s * FFN_SLAB, D_FF + (s + 1) * FFN_SLAB)
        g_ref[hist:hist + m, cols] = (
            jnp.dot(xb_ref[...], wup_ref[:, cols], preferred_element_type=F32)
            + bup_ref[:, cols])
        v_ref[:, cols] = (
            jnp.dot(xb_ref[...], wup_ref[:, vcols], preferred_element_type=F32)
            + bup_ref[:, vcols])
        for rows in row_blocks:
            acc = convb_ref[:, cols] + convw_ref[0:1, cols] * g_ref[rows, cols]
            for j in range(1, FFN_CONV):
                shifted = pl.ds(rows.start + j * nb, rb)
                acc = acc + convw_ref[j:j + 1, cols] * g_ref[shifted, cols]
            hb_ref[rows, cols] = (_gelu(acc) * v_ref[rows, cols]).astype(BF16)
        part = jnp.dot(hb_ref[:, cols], wdown_ref[cols, :], preferred_element_type=F32)
        if s == 0:
            t_ref[...] = part + bdown_ref[...]
        else:
            t_ref[...] += part

    for rows in row_blocks:
        out = _layer_norm(ALPHA * x_ref[rows, :] + t_ref[rows, :], ln2g_ref[...], ln2b_ref[...])
        if len(o_ref.shape) == 3:
            for k in range(rb // nb):
                o_ref[:, rows.start // nb + k, :] = out[k * nb:(k + 1) * nb, :]
        else:
            o_ref[rows, :] = out

    tail = g_ref[m:m + hist, :]
    g_ref[0:hist, :] = tail
    gnew_ref[...] = tail


def _ffn(x, g0, params, *, nb, tc, batch_major_out=False):
    m_total = x.shape[0]
    m = nb * tc
    assert m_total % m == 0
    hist = (FFN_CONV - 1) * nb
    row_block = pl.BlockSpec((m, D_MODEL), lambda i: (i, 0))
    in_specs = [row_block] + [_whole(None)] * (1 + len(params))
    if batch_major_out:
        y_shape = jax.ShapeDtypeStruct((nb, m_total // nb, D_MODEL), F32)
        y_spec = pl.BlockSpec((nb, tc, D_MODEL), lambda i: (0, i, 0))
    else:
        y_shape = jax.ShapeDtypeStruct((m_total, D_MODEL), F32)
        y_spec = row_block
    out_shape = (y_shape, jax.ShapeDtypeStruct((hist, D_FF), F32))
    out_specs = (y_spec, pl.BlockSpec((hist, D_FF), lambda i: (0, 0)))
    scratch = [
        pltpu.VMEM((m, D_MODEL), BF16),
        pltpu.VMEM((hist + m, D_FF), F32),
        pltpu.VMEM((m, D_FF), F32),
        pltpu.VMEM((m, D_FF), BF16),
        pltpu.VMEM((m, D_MODEL), F32),
    ]
    return pl.pallas_call(
        functools.partial(_ffn_kernel, nb, tc),
        grid=(m_total // m,),
        in_specs=in_specs,
        out_specs=out_specs,
        out_shape=out_shape,
        scratch_shapes=scratch,
        compiler_params=pltpu.CompilerParams(
            dimension_semantics=("arbitrary",), vmem_limit_bytes=VMEM_LIMIT_BYTES),
        name=f"ffn_nb{nb}_tc{tc}",
    )(x, g0, *params)


def _block_diag_mask(rows_per_group, cols_per_group):
    r = jnp.arange(SSM_GROUPS * rows_per_group)[:, None] // rows_per_group
    c = jnp.arange(SSM_GROUPS * cols_per_group)[None, :] // cols_per_group
    return (r == c).astype(F32)


def _ssm_matrices(bb_re_t, bb_im_t, c_re, c_im):
    mask_b = _block_diag_mask(SSM_GROUP, SSM_STATE)
    wb_re = jnp.tile(bb_re_t, (SSM_GROUPS, 1)) * mask_b
    wb_im = jnp.tile(bb_im_t, (SSM_GROUPS, 1)) * mask_b
    kb = D_SSM // SSM_SLABS
    nbk = N_STATE // SSM_SLABS
    wb = jnp.stack([
        jnp.concatenate([wb_re[s * kb:(s + 1) * kb, s * nbk:(s + 1) * nbk],
                         wb_im[s * kb:(s + 1) * kb, s * nbk:(s + 1) * nbk]], axis=1)
        for s in range(SSM_SLABS)]).astype(BF16)

    mask_c = _block_diag_mask(SSM_STATE, SSM_GROUP)

    def c_dense(c):
        ct = c.transpose(0, 2, 1).reshape(N_STATE, SSM_GROUP)
        return jnp.tile(ct, (1, SSM_GROUPS)) * mask_c

    def c_slabs(c):
        cd = c_dense(c)
        return jnp.stack([cd[s * nbk:(s + 1) * nbk, s * kb:(s + 1) * kb]
                          for s in range(SSM_SLABS)]).astype(BF16)

    return wb, c_slabs(c_re), c_slabs(c_im)


def _to_time_major(state):
    nb, k, c = state.shape
    return state.transpose(1, 0, 2).reshape(k * nb, c)


def _from_time_major(state, nb):
    k = state.shape[0] // nb
    return state.reshape(k, nb, state.shape[1]).transpose(1, 0, 2)


def kernel(x_prompt, x_sample, cache_conv_lru, state_lru, state_ssm_re, state_ssm_im, cache_conv_ffn,
           meta_tokens, emb_ln_g, emb_ln_b, w_in, b_in, conv_lru_w, conv_lru_b,
           lru_w_r, lru_b_r, lru_w_i, lru_b_i, lru_lambda,
           ssm_lambda_re, ssm_lambda_im, ssm_log_dt, ssm_b_re, ssm_b_im, ssm_c_re, ssm_c_im, ssm_d,
           w_glu, b_glu, w_out, b_out, ln1_g, ln1_b,
           w_up, b_up, ffn_conv_w, ffn_conv_b, w_down, b_down, ln2_g, ln2_b):
    assert w_in.shape[0] == DEPTH
    nbp, seq, _ = x_prompt.shape
    nbs = x_sample.shape[0]
    row = lambda a: a.reshape(1, -1).astype(F32)

    ab_re, ab_im, bb_re_t, bb_im_t = _ssm_prep(ssm_lambda_re[0], ssm_lambda_im[0], ssm_log_dt[0],
                                                ssm_b_re[0], ssm_b_im[0])
    wb, wc_re, wc_im = _ssm_matrices(bb_re_t, bb_im_t, ssm_c_re[0], ssm_c_im[0])
    wri = jnp.concatenate([lru_w_r[0], lru_w_i[0]], axis=-1).astype(BF16)

    mixer_params = (row(emb_ln_g), row(emb_ln_b), w_in[0].astype(BF16), row(b_in[0]),
                    conv_lru_w[0], row(conv_lru_b[0]),
                    wri, row(lru_b_r[0]), row(lru_b_i[0]), row(lru_lambda[0]),
                    ab_re, ab_im, wb, wc_re, wc_im, row(ssm_d[0]),
                    w_glu[0].astype(BF16), row(b_glu[0]), w_out[0].astype(BF16), row(b_out[0]),
                    row(ln1_g[0]), row(ln1_b[0]))
    ffn_params = (w_up[0].astype(BF16), row(b_up[0]), ffn_conv_w[0], row(ffn_conv_b[0]),
                  w_down[0].astype(BF16), row(b_down[0]), row(ln2_g[0]), row(ln2_b[0]))

    def layer(x, conv0, h0, sre0, sim0, g0, nb, tc):
        x1, conv, h, sre, sim = _mixer(x, conv0, h0, sre0, sim0, mixer_params, nb=nb, tc=tc)
        x2, g = _ffn(x1, g0, ffn_params, nb=nb, tc=tc, batch_major_out=(x.ndim == 3))
        return x2, (conv, h, sre, sim, g)

    zeros = lambda r, c: jnp.zeros((r, c), F32)
    meta = jnp.broadcast_to(meta_tokens[:, None, :].astype(F32), (N_META, nbp, D_MODEL))
    _, st = layer(meta.reshape(N_META * nbp, D_MODEL),
                  zeros((LRU_CONV - 1) * nbp, D_LRU), zeros(nbp, D_LRU),
                  zeros(nbp, N_STATE), zeros(nbp, N_STATE), zeros((FFN_CONV - 1) * nbp, D_FF),
                  nbp, N_META)
    y_prompt, st_p = layer(x_prompt, *st, nbp, PROMPT_STEPS_PER_CHUNK)

    ys, st_s = layer(x_sample.reshape(nbs, D_MODEL),
                     _to_time_major(cache_conv_lru[0]), state_lru[0],
                     state_ssm_re[0].reshape(nbs, N_STATE), state_ssm_im[0].reshape(nbs, N_STATE),
                     _to_time_major(cache_conv_ffn[0]), nbs, 1)
    y_sample = ys.reshape(nbs, 1, D_MODEL)

    def states(st, nb):
        conv, h, sre, sim, g = st
        return (_from_time_major(conv, nb)[None], h[None],
                sre.reshape(1, nb, SSM_GROUPS, SSM_STATE), sim.reshape(1, nb, SSM_GROUPS, SSM_STATE),
                _from_time_major(g, nb)[None])

    return (y_prompt, y_sample) + states(st_p, nbp) + states(st_s, nbs)
```

```python
import functools
import math

import jax
import jax.numpy as jnp
from jax import lax
from jax.experimental import pallas as pl
from jax.experimental.pallas import tpu as pltpu

D_MODEL = 1024
N_META = 16
D_LRU = D_MODEL
LRU_BLOCKS = 8
LRU_BLOCK = D_LRU // LRU_BLOCKS
LRU_CONV = 4
LRU_C = 8.0
D_SSM = D_MODEL // 2
SSM_GROUP = 16
SSM_GROUPS = D_SSM // SSM_GROUP
SSM_STATE = 64
N_STATE = SSM_GROUPS * SSM_STATE
D_FF = 3 * D_MODEL
FFN_CONV = 3
D_IN = D_LRU + D_SSM + 2 * D_MODEL
DEPTH = 1
ALPHA = (2.0 * DEPTH) ** 0.25
LN_EPS = 1e-5
GELU_C = math.sqrt(2.0 / math.pi)

V7X_VMEM_BYTES = 64 * 1024 * 1024
VMEM_LIMIT_BYTES = V7X_VMEM_BYTES - 8 * 1024 * 1024
SUBLANES = 8
ROW_BLOCK = 64
SSM_SLABS = 2
SSM_LANE_BLOCK = 512
PROMPT_STEPS_PER_CHUNK = 64
FFN_SLAB = 1024

BF16 = jnp.bfloat16
F32 = jnp.float32


def _layer_norm(x, g, b):
    mu = jnp.mean(x, axis=-1, keepdims=True)
    xc = x - mu
    var = jnp.mean(xc * xc, axis=-1, keepdims=True)
    return xc * lax.rsqrt(var + LN_EPS) * g + b


def _gelu(x):
    return 0.5 * x * (1.0 + jnp.tanh(GELU_C * (x + 0.044715 * (x * x * x))))


def _sigmoid(x):
    return jax.nn.sigmoid(x)


def _softplus(x):
    return jnp.maximum(x, 0.0) + jnp.log1p(jnp.exp(-jnp.abs(x)))


def _for_row_blocks(n_rows, fn):
    rb = min(ROW_BLOCK, n_rows)
    assert n_rows % rb == 0
    for off in range(0, n_rows, rb):
        fn(off, rb)


def _ssm_prep_kernel(lre_ref, lim_ref, logdt_ref, bre_ref, bim_ref,
                     abre_ref, abim_ref, bbre_ref, bbim_ref):
    lre = lre_ref[...]
    lim = lim_ref[...]
    dt = jnp.exp(logdt_ref[...])
    mag = jnp.exp(lre * dt)
    ab_re = mag * jnp.cos(lim * dt)
    ab_im = mag * jnp.sin(lim * dt)
    den = lre * lre + lim * lim
    nr = ab_re - 1.0
    ni = ab_im
    f_re = (nr * lre + ni * lim) / den
    f_im = (ni * lre - nr * lim) / den
    br = bre_ref[...]
    bi = bim_ref[...]
    abre_ref[...] = ab_re
    abim_ref[...] = ab_im
    bbre_ref[...] = f_re * br - f_im * bi
    bbim_ref[...] = f_re * bi + f_im * br


def _ssm_prep(lam_re, lam_im, log_dt, b_re, b_im):
    row = lambda a: a.reshape(1, N_STATE)
    b_t = lambda a: a.transpose(2, 0, 1).reshape(SSM_GROUP, N_STATE)
    logdt = jnp.repeat(log_dt, SSM_STATE).reshape(1, N_STATE)
    out_shape = (jax.ShapeDtypeStruct((1, N_STATE), F32),) * 2 + (
        jax.ShapeDtypeStruct((SSM_GROUP, N_STATE), F32),) * 2
    return pl.pallas_call(_ssm_prep_kernel, out_shape=out_shape, name="ssm_prep")(
        row(lam_re), row(lam_im), logdt, b_t(b_re), b_t(b_im))


def _mixer_kernel(nb, tc,
                  x_ref, conv0_ref, h0_ref, sre0_ref, sim0_ref,
                  embg_ref, embb_ref, win_ref, bin_ref, convw_ref, convb_ref,
                  wri_ref, br_ref, bi_ref, lam_ref,
                  abre_ref, abim_ref, wb_ref, wcre_ref, wcim_ref, ssmd_ref,
                  wglu_ref, bglu_ref, wout_ref, bout_ref, ln1g_ref, ln1b_ref,
                  o_ref, conv_ref, h_ref, sre_ref, sim_ref,
                  zb_ref, xa_ref, us_ref, ga_ref, gb_ref, t1_ref, cb_ref, p_ref, q_ref,
                  ya_ref, y_ref, glb_ref):
    m = nb * tc
    hist = (LRU_CONV - 1) * nb
    step = pl.program_id(0)

    @pl.when(step == 0)
    def _():
        xa_ref[0:hist, :] = conv0_ref[...]
        h_ref[...] = h0_ref[...]
        sre_ref[...] = sre0_ref[...]
        sim_ref[...] = sim0_ref[...]

    if len(x_ref.shape) == 3:
        for t in range(tc):
            o_ref[pl.ds(t * nb, nb), :] = x_ref[:, t, :]
        src_ref = o_ref
    else:
        src_ref = x_ref

    def ln_in(off, rb):
        z = _layer_norm(src_ref[pl.ds(off, rb), :], embg_ref[...], embb_ref[...])
        o_ref[pl.ds(off, rb), :] = z
        zb_ref[pl.ds(off, rb), :] = z.astype(BF16)

    _for_row_blocks(m, ln_in)

    def proj(lo, hi):
        return (jnp.dot(zb_ref[...], win_ref[:, lo:hi], preferred_element_type=F32)
                + bin_ref[:, lo:hi])

    c0, c1, c2 = D_LRU, D_LRU + D_SSM, D_LRU + D_SSM + D_MODEL
    us_ref[...] = proj(c0, c1)

    kb = D_SSM // SSM_SLABS
    nbk = N_STATE // SSM_SLABS
    for s in range(SSM_SLABS):
        bu = jnp.dot(us_ref[:, s * kb:(s + 1) * kb].astype(BF16), wb_ref[s],
                     preferred_element_type=F32)
        p_ref[:, s * nbk:(s + 1) * nbk] = bu[:, :nbk]
        q_ref[:, s * nbk:(s + 1) * nbk] = bu[:, nbk:]

    xa_ref[hist:hist + m, :] = proj(0, c0)

    if tc == 1:
        def ssm_step(off, rb):
            rows = pl.ds(off, rb)
            ar, ai = abre_ref[...], abim_ref[...]
            hr, hi = sre0_ref[rows, :], sim0_ref[rows, :]
            nr = ar * hr - ai * hi + p_ref[rows, :]
            ni = ar * hi + ai * hr + q_ref[rows, :]
            p_ref[rows, :] = nr
            q_ref[rows, :] = ni
            sre_ref[rows, :] = nr
            sim_ref[rows, :] = ni

        for off in range(0, m, SUBLANES):
            ssm_step(off, SUBLANES)
    else:
        assert nb == SUBLANES
        for lb in range(N_STATE // SSM_LANE_BLOCK):
            lanes = slice(lb * SSM_LANE_BLOCK, (lb + 1) * SSM_LANE_BLOCK)
            ar = jnp.broadcast_to(abre_ref[:, lanes], (nb, SSM_LANE_BLOCK))
            ai = jnp.broadcast_to(abim_ref[:, lanes], (nb, SSM_LANE_BLOCK))
            hr, hi = sre_ref[:, lanes], sim_ref[:, lanes]
            for t in range(tc):
                rows = pl.ds(t * nb, nb)
                hr, hi = (ar * hr - ai * hi + p_ref[rows, lanes],
                          ar * hi + ai * hr + q_ref[rows, lanes])
                p_ref[rows, lanes] = hr
                q_ref[rows, lanes] = hi
            sre_ref[:, lanes] = hr
            sim_ref[:, lanes] = hi

    def conv(off, rb):
        acc = convb_ref[...] + convw_ref[0:1, :] * xa_ref[pl.ds(off, rb), :]
        for j in range(1, LRU_CONV):
            acc = acc + convw_ref[j:j + 1, :] * xa_ref[pl.ds(off + j * nb, rb), :]
        t1_ref[pl.ds(off, rb), :] = acc
        cb_ref[pl.ds(off, rb), :] = acc.astype(BF16)

    _for_row_blocks(m, conv)

    ga_ref[...] = proj(c1, c2)

    sp = _softplus(-lam_ref[...])
    rb = min(ROW_BLOCK, m)
    for n in range(LRU_BLOCKS):
        cols = slice(n * LRU_BLOCK, (n + 1) * LRU_BLOCK)
        ri = jnp.dot(cb_ref[:, cols], wri_ref[n], preferred_element_type=F32)
        h = None if tc == 1 else h_ref[:, cols]
        for off in range(0, m, rb):
            rows = pl.ds(off, rb)
            r = _sigmoid(ri[off:off + rb, :LRU_BLOCK] + br_ref[:, cols])
            i = _sigmoid(ri[off:off + rb, LRU_BLOCK:] + bi_ref[:, cols])
            log_a = (-LRU_C * r) * sp[:, cols]
            a = jnp.exp(log_a)
            mult = jnp.sqrt(-jnp.tanh(log_a) * (a * a + 1.0))
            bx = mult * (i * t1_ref[rows, cols])
            if tc == 1:
                hn = a * h0_ref[rows, cols] + bx
                ya_ref[rows, cols] = hn
                h_ref[rows, cols] = hn
            else:
                for k in range(0, rb, nb):
                    h = a[k:k + nb, :] * h + bx[k:k + nb, :]
                    ya_ref[pl.ds(off + k, nb), cols] = h
        if tc != 1:
            h_ref[:, cols] = h

    gb_ref[...] = proj(c2, D_IN)

    kc = D_SSM // SSM_SLABS
    for s in range(SSM_SLABS):
        hs = slice(s * nbk, (s + 1) * nbk)
        y_ref[:, s * kc:(s + 1) * kc] = (
            jnp.dot(p_ref[:, hs].astype(BF16), wcre_ref[s], preferred_element_type=F32)
            - jnp.dot(q_ref[:, hs].astype(BF16), wcim_ref[s], preferred_element_type=F32))

    def ssm_out(off, rb):
        rows = pl.ds(off, rb)
        ys = y_ref[rows, :] + ssmd_ref[...] * us_ref[rows, :]
        glb_ref[rows, :] = _gelu(ys).astype(BF16)

    _for_row_blocks(m, ssm_out)

    p_ref[...] = (jnp.dot(glb_ref[...], wglu_ref[...], preferred_element_type=F32)
                  + bglu_ref[...])

    def merge(off, rb):
        rows = pl.ds(off, rb)
        yb = p_ref[rows, 0:D_MODEL] * _sigmoid(p_ref[rows, D_MODEL:2 * D_MODEL])
        mix = _sigmoid(ga_ref[rows, :]) * ya_ref[rows, :] + _sigmoid(gb_ref[rows, :]) * yb
        cb_ref[rows, :] = mix.astype(BF16)

    _for_row_blocks(m, merge)

    t1_ref[...] = (jnp.dot(cb_ref[...], wout_ref[...], preferred_element_type=F32)
                   + bout_ref[...])

    def ln_out(off, rb):
        rows = pl.ds(off, rb)
        o_ref[rows, :] = _layer_norm(ALPHA * o_ref[rows, :] + t1_ref[rows, :],
                                     ln1g_ref[...], ln1b_ref[...])

    _for_row_blocks(m, ln_out)

    tail = xa_ref[m:m + hist, :]
    xa_ref[0:hist, :] = tail
    conv_ref[...] = tail


def _whole(shape):
    return pl.BlockSpec(memory_space=pltpu.VMEM)


def _mixer(x, conv0, h0, sre0, sim0, params, *, nb, tc):
    m = nb * tc
    row_block = pl.BlockSpec((m, D_MODEL), lambda i: (i, 0))
    if x.ndim == 3:
        assert x.shape[0] == nb and x.shape[1] % tc == 0
        m_total = nb * x.shape[1]
        x_spec = pl.BlockSpec((nb, tc, D_MODEL), lambda i: (0, i, 0))
    else:
        m_total = x.shape[0]
        x_spec = row_block
    assert m_total % m == 0
    hist = (LRU_CONV - 1) * nb
    in_specs = [x_spec] + [_whole(None)] * (4 + len(params))
    out_shape = (jax.ShapeDtypeStruct((m_total, D_MODEL), F32),
                 jax.ShapeDtypeStruct((hist, D_LRU), F32),
                 jax.ShapeDtypeStruct((nb, D_LRU), F32),
                 jax.ShapeDtypeStruct((nb, N_STATE), F32),
                 jax.ShapeDtypeStruct((nb, N_STATE), F32))
    out_specs = (row_block,
                 pl.BlockSpec((hist, D_LRU), lambda i: (0, 0)),
                 pl.BlockSpec((nb, D_LRU), lambda i: (0, 0)),
                 pl.BlockSpec((nb, N_STATE), lambda i: (0, 0)),
                 pl.BlockSpec((nb, N_STATE), lambda i: (0, 0)))
    scratch = [
        pltpu.VMEM((m, D_MODEL), BF16),
        pltpu.VMEM((hist + m, D_LRU), F32),
        pltpu.VMEM((m, D_SSM), F32),
        pltpu.VMEM((m, D_MODEL), F32),
        pltpu.VMEM((m, D_MODEL), F32),
        pltpu.VMEM((m, D_MODEL), F32),
        pltpu.VMEM((m, D_MODEL), BF16),
        pltpu.VMEM((m, N_STATE), F32),
        pltpu.VMEM((m, N_STATE), F32),
        pltpu.VMEM((m, D_LRU), F32),
        pltpu.VMEM((m, D_SSM), F32),
        pltpu.VMEM((m, D_SSM), BF16),
    ]
    return pl.pallas_call(
        functools.partial(_mixer_kernel, nb, tc),
        grid=(m_total // m,),
        in_specs=in_specs,
        out_specs=out_specs,
        out_shape=out_shape,
        scratch_shapes=scratch,
        compiler_params=pltpu.CompilerParams(
            dimension_semantics=("arbitrary",), vmem_limit_bytes=VMEM_LIMIT_BYTES),
        name=f"mixer_nb{nb}_tc{tc}",
    )(x, conv0, h0, sre0, sim0, *params)


def _ffn_kernel(nb, tc,
                x_ref, g0_ref, wup_ref, bup_ref, convw_ref, convb_ref, wdown_ref, bdown_ref,
                ln2g_ref, ln2b_ref,
                o_ref, gnew_ref,
                xb_ref, g_ref, v_ref, hb_ref, t_ref):
    m = nb * tc
    hist = (FFN_CONV - 1) * nb
    step = pl.program_id(0)
    rb = min(ROW_BLOCK, m)
    row_blocks = [pl.ds(off, rb) for off in range(0, m, rb)]

    @pl.when(step == 0)
    def _():
        g_ref[0:hist, :] = g0_ref[...]

    for rows in row_blocks:
        xb_ref[rows, :] = x_ref[rows, :].astype(BF16)

    for s in range(D_FF // FFN_SLAB):
        cols = slice(s * FFN_SLAB, (s + 1) * FFN_SLAB)
        vcols = slice(D_FF + s * FFN_SLAB, D_FF + (s + 1) * FFN_SLAB)
        g_ref[hist:hist + m, cols] = (
            jnp.dot(xb_ref[...], wup_ref[:, cols], preferred_element_type=F32)
            + bup_ref[:, cols])
        v_ref[:, cols] = (
            jnp.dot(xb_ref[...], wup_ref[:, vcols], preferred_element_type=F32)
            + bup_ref[:, vcols])
        for rows in row_blocks:
            acc = convb_ref[:, cols] + convw_ref[0:1, cols] * g_ref[rows, cols]
            for j in range(1, FFN_CONV):
                shifted = pl.ds(rows.start + j * nb, rb)
                acc = acc + convw_ref[j:j + 1, cols] * g_ref[shifted, cols]
            hb_ref[rows, cols] = (_gelu(acc) * v_ref[rows, cols]).astype(BF16)
        part = jnp.dot(hb_ref[:, cols], wdown_ref[cols, :], preferred_element_type=F32)
        if s == 0:
            t_ref[...] = part + bdown_ref[...]
        else:
            t_ref[...] += part

    for rows in row_blocks:
        out = _layer_norm(ALPHA * x_ref[rows, :] + t_ref[rows, :], ln2g_ref[...], ln2b_ref[...])
        if len(o_ref.shape) == 3:
            for k in range(rb // nb):
                o_ref[:, rows.start // nb + k, :] = out[k * nb:(k + 1) * nb, :]
        else:
            o_ref[rows, :] = out

    tail = g_ref[m:m + hist, :]
    g_ref[0:hist, :] = tail
    gnew_ref[...] = tail


def _ffn(x, g0, params, *, nb, tc, batch_major_out=False):
    m_total = x.shape[0]
    m = nb * tc
    assert m_total % m == 0
    hist = (FFN_CONV - 1) * nb
    row_block = pl.BlockSpec((m, D_MODEL), lambda i: (i, 0))
    in_specs = [row_block] + [_whole(None)] * (1 + len(params))
    if batch_major_out:
        y_shape = jax.ShapeDtypeStruct((nb, m_total // nb, D_MODEL), F32)
        y_spec = pl.BlockSpec((nb, tc, D_MODEL), lambda i: (0, i, 0))
    else:
        y_shape = jax.ShapeDtypeStruct((m_total, D_MODEL), F32)
        y_spec = row_block
    out_shape = (y_shape, jax.ShapeDtypeStruct((hist, D_FF), F32))
    out_specs = (y_spec, pl.BlockSpec((hist, D_FF), lambda i: (0, 0)))
    scratch = [
        pltpu.VMEM((m, D_MODEL), BF16),
        pltpu.VMEM((hist + m, D_FF), F32),
        pltpu.VMEM((m, D_FF), F32),
        pltpu.VMEM((m, D_FF), BF16),
        pltpu.VMEM((m, D_MODEL), F32),
    ]
    return pl.pallas_call(
        functools.partial(_ffn_kernel, nb, tc),
        grid=(m_total // m,),
        in_specs=in_specs,
        out_specs=out_specs,
        out_shape=out_shape,
        scratch_shapes=scratch,
        compiler_params=pltpu.CompilerParams(
            dimension_semantics=("arbitrary",), vmem_limit_bytes=VMEM_LIMIT_BYTES),
        name=f"ffn_nb{nb}_tc{tc}",
    )(x, g0, *params)


def _block_diag_mask(rows_per_group, cols_per_group):
    r = jnp.arange(SSM_GROUPS * rows_per_group)[:, None] // rows_per_group
    c = jnp.arange(SSM_GROUPS * cols_per_group)[None, :] // cols_per_group
    return (r == c).astype(F32)


def _ssm_matrices(bb_re_t, bb_im_t, c_re, c_im):
    mask_b = _block_diag_mask(SSM_GROUP, SSM_STATE)
    wb_re = jnp.tile(bb_re_t, (SSM_GROUPS, 1)) * mask_b
    wb_im = jnp.tile(bb_im_t, (SSM_GROUPS, 1)) * mask_b
    kb = D_SSM // SSM_SLABS
    nbk = N_STATE // SSM_SLABS
    wb = jnp.stack([
        jnp.concatenate([wb_re[s * kb:(s + 1) * kb, s * nbk:(s + 1) * nbk],
                         wb_im[s * kb:(s + 1) * kb, s * nbk:(s + 1) * nbk]], axis=1)
        for s in range(SSM_SLABS)]).astype(BF16)

    mask_c = _block_diag_mask(SSM_STATE, SSM_GROUP)

    def c_dense(c):
        ct = c.transpose(0, 2, 1).reshape(N_STATE, SSM_GROUP)
        return jnp.tile(ct, (1, SSM_GROUPS)) * mask_c

    def c_slabs(c):
        cd = c_dense(c)
        return jnp.stack([cd[s * nbk:(s + 1) * nbk, s * kb:(s + 1) * kb]
                          for s in range(SSM_SLABS)]).astype(BF16)

    return wb, c_slabs(c_re), c_slabs(c_im)


def _to_time_major(state):
    nb, k, c = state.shape
    return state.transpose(1, 0, 2).reshape(k * nb, c)


def _from_time_major(state, nb):
    k = state.shape[0] // nb
    return state.reshape(k, nb, state.shape[1]).transpose(1, 0, 2)


def kernel(x_prompt, x_sample, cache_conv_lru, state_lru, state_ssm_re, state_ssm_im, cache_conv_ffn,
           meta_tokens, emb_ln_g, emb_ln_b, w_in, b_in, conv_lru_w, conv_lru_b,
           lru_w_r, lru_b_r, lru_w_i, lru_b_i, lru_lambda,
           ssm_lambda_re, ssm_lambda_im, ssm_log_dt, ssm_b_re, ssm_b_im, ssm_c_re, ssm_c_im, ssm_d,
           w_glu, b_glu, w_out, b_out, ln1_g, ln1_b,
           w_up, b_up, ffn_conv_w, ffn_conv_b, w_down, b_down, ln2_g, ln2_b):
    assert w_in.shape[0] == DEPTH
    nbp, seq, _ = x_prompt.shape
    nbs = x_sample.shape[0]
    row = lambda a: a.reshape(1, -1).astype(F32)

    ab_re, ab_im, bb_re_t, bb_im_t = _ssm_prep(ssm_lambda_re[0], ssm_lambda_im[0], ssm_log_dt[0],
                                                ssm_b_re[0], ssm_b_im[0])
    wb, wc_re, wc_im = _ssm_matrices(bb_re_t, bb_im_t, ssm_c_re[0], ssm_c_im[0])
    wri = jnp.concatenate([lru_w_r[0], lru_w_i[0]], axis=-1).astype(BF16)

    mixer_params = (row(emb_ln_g), row(emb_ln_b), w_in[0].astype(BF16), row(b_in[0]),
                    conv_lru_w[0], row(conv_lru_b[0]),
                    wri, row(lru_b_r[0]), row(lru_b_i[0]), row(lru_lambda[0]),
                    ab_re, ab_im, wb, wc_re, wc_im, row(ssm_d[0]),
                    w_glu[0].astype(BF16), row(b_glu[0]), w_out[0].astype(BF16), row(b_out[0]),
                    row(ln1_g[0]), row(ln1_b[0]))
    ffn_params = (w_up[0].astype(BF16), row(b_up[0]), ffn_conv_w[0], row(ffn_conv_b[0]),
                  w_down[0].astype(BF16), row(b_down[0]), row(ln2_g[0]), row(ln2_b[0]))

    def layer(x, conv0, h0, sre0, sim0, g0, nb, tc):
        x1, conv, h, sre, sim = _mixer(x, conv0, h0, sre0, sim0, mixer_params, nb=nb, tc=tc)
        x2, g = _ffn(x1, g0, ffn_params, nb=nb, tc=tc, batch_major_out=(x.ndim == 3))
        return x2, (conv, h, sre, sim, g)

    zeros = lambda r, c: jnp.zeros((r, c), F32)
    meta = jnp.broadcast_to(meta_tokens[:, None, :].astype(F32), (N_META, nbp, D_MODEL))
    _, st = layer(meta.reshape(N_META * nbp, D_MODEL),
                  zeros((LRU_CONV - 1) * nbp, D_LRU), zeros(nbp, D_LRU),
                  zeros(nbp, N_STATE), zeros(nbp, N_STATE), zeros((FFN_CONV - 1) * nbp, D_FF),
                  nbp, N_META)
    y_prompt, st_p = layer(x_prompt, *st, nbp, PROMPT_STEPS_PER_CHUNK)

    ys, st_s = layer(x_sample.reshape(nbs, D_MODEL),
                     _to_time_major(cache_conv_lru[0]), state_lru[0],
                     state_ssm_re[0].reshape(nbs, N_STATE), state_ssm_im[0].reshape(nbs, N_STATE),
                     _to_time_major(cache_conv_ffn[0]), nbs, 1)
    y_sample = ys.reshape(nbs, 1, D_MODEL)

    def states(st, nb):
        conv, h, sre, sim, g = st
        return (_from_time_major(conv, nb)[None], h[None],
                sre.reshape(1, nb, SSM_GROUPS, SSM_STATE), sim.reshape(1, nb, SSM_GROUPS, SSM_STATE),
                _from_time_major(g, nb)[None])

    return (y_prompt, y_sample) + states(st_p, nbp) + states(st_s, nbs)
```

```python
import functools
import math

import jax
import jax.numpy as jnp
from jax import lax
from jax.experimental import pallas as pl
from jax.experimental.pallas import tpu as pltpu

D_MODEL = 1024
N_META = 16
D_LRU = D_MODEL
LRU_BLOCKS = 8
LRU_BLOCK = D_LRU // LRU_BLOCKS
LRU_CONV = 4
LRU_C = 8.0
D_SSM = D_MODEL // 2
SSM_GROUP = 16
SSM_GROUPS = D_SSM // SSM_GROUP
SSM_STATE = 64
N_STATE = SSM_GROUPS * SSM_STATE
D_FF = 3 * D_MODEL
FFN_CONV = 3
D_IN = D_LRU + D_SSM + 2 * D_MODEL
DEPTH = 1
ALPHA = (2.0 * DEPTH) ** 0.25
LN_EPS = 1e-5
GELU_C = math.sqrt(2.0 / math.pi)

V7X_VMEM_BYTES = 64 * 1024 * 1024
VMEM_LIMIT_BYTES = V7X_VMEM_BYTES - 8 * 1024 * 1024
SUBLANES = 8
ROW_BLOCK = 64
SSM_SLABS = 2
SSM_LANE_BLOCK = 512
PROMPT_STEPS_PER_CHUNK = 64
FFN_SLAB = 1024

BF16 = jnp.bfloat16
F32 = jnp.float32


def _layer_norm(x, g, b):
    mu = jnp.mean(x, axis=-1, keepdims=True)
    xc = x - mu
    var = jnp.mean(xc * xc, axis=-1, keepdims=True)
    return xc * lax.rsqrt(var + LN_EPS) * g + b


def _gelu(x):
    return 0.5 * x * (1.0 + jnp.tanh(GELU_C * (x + 0.044715 * (x * x * x))))


def _sigmoid(x):
    return jax.nn.sigmoid(x)


def _softplus(x):
    return jnp.maximum(x, 0.0) + jnp.log1p(jnp.exp(-jnp.abs(x)))


def _for_row_blocks(n_rows, fn):
    rb = min(ROW_BLOCK, n_rows)
    assert n_rows % rb == 0
    for off in range(0, n_rows, rb):
        fn(off, rb)


def _ssm_prep_kernel(lre_ref, lim_ref, logdt_ref, bre_ref, bim_ref,
                     abre_ref, abim_ref, bbre_ref, bbim_ref):
    lre = lre_ref[...]
    lim = lim_ref[...]
    dt = jnp.exp(logdt_ref[...])
    mag = jnp.exp(lre * dt)
    ab_re = mag * jnp.cos(lim * dt)
    ab_im = mag * jnp.sin(lim * dt)
    den = lre * lre + lim * lim
    nr = ab_re - 1.0
    ni = ab_im
    f_re = (nr * lre + ni * lim) / den
    f_im = (ni * lre - nr * lim) / den
    br = bre_ref[...]
    bi = bim_ref[...]
    abre_ref[...] = ab_re
    abim_ref[...] = ab_im
    bbre_ref[...] = f_re * br - f_im * bi
    bbim_ref[...] = f_re * bi + f_im * br


def _ssm_prep(lam_re, lam_im, log_dt, b_re, b_im):
    row = lambda a: a.reshape(1, N_STATE)
    b_t = lambda a: a.transpose(2, 0, 1).reshape(SSM_GROUP, N_STATE)
    logdt = jnp.repeat(log_dt, SSM_STATE).reshape(1, N_STATE)
    out_shape = (jax.ShapeDtypeStruct((1, N_STATE), F32),) * 2 + (
        jax.ShapeDtypeStruct((SSM_GROUP, N_STATE), F32),) * 2
    return pl.pallas_call(_ssm_prep_kernel, out_shape=out_shape, name="ssm_prep")(
        row(lam_re), row(lam_im), logdt, b_t(b_re), b_t(b_im))


def _mixer_kernel(nb, tc,
                  x_ref, conv0_ref, h0_ref, sre0_ref, sim0_ref,
                  embg_ref, embb_ref, win_ref, bin_ref, convw_ref, convb_ref,
                  wri_ref, br_ref, bi_ref, lam_ref,
                  abre_ref, abim_ref, wb_ref, wcre_ref, wcim_ref, ssmd_ref,
                  wglu_ref, bglu_ref, wout_ref, bout_ref, ln1g_ref, ln1b_ref,
                  o_ref, conv_ref, h_ref, sre_ref, sim_ref,
                  zb_ref, xa_ref, us_ref, ga_ref, gb_ref, t1_ref, cb_ref, p_ref, q_ref,
                  ya_ref, y_ref, glb_ref):
    m = nb * tc
    hist = (LRU_CONV - 1) * nb
    step = pl.program_id(0)

    @pl.when(step == 0)
    def _():
        xa_ref[0:hist, :] = conv0_ref[...]
        h_ref[...] = h0_ref[...]
        sre_ref[...] = sre0_ref[...]
        sim_ref[...] = sim0_ref[...]

    if len(x_ref.shape) == 3:
        for t in range(tc):
            o_ref[pl.ds(t * nb, nb), :] = x_ref[:, t, :]
        src_ref = o_ref
    else:
        src_ref = x_ref

    def ln_in(off, rb):
        z = _layer_norm(src_ref[pl.ds(off, rb), :], embg_ref[...], embb_ref[...])
        o_ref[pl.ds(off, rb), :] = z
        zb_ref[pl.ds(off, rb), :] = z.astype(BF16)

    _for_row_blocks(m, ln_in)

    def proj(lo, hi):
        return (jnp.dot(zb_ref[...], win_ref[:, lo:hi], preferred_element_type=F32)
                + bin_ref[:, lo:hi])

    c0, c1, c2 = D_LRU, D_LRU + D_SSM, D_LRU + D_SSM + D_MODEL
    us_ref[...] = proj(c0, c1)

    kb = D_SSM // SSM_SLABS
    nbk = N_STATE // SSM_SLABS
    for s in range(SSM_SLABS):
        bu = jnp.dot(us_ref[:, s * kb:(s + 1) * kb].astype(BF16), wb_ref[s],
                     preferred_element_type=F32)
        p_ref[:, s * nbk:(s + 1) * nbk] = bu[:, :nbk]
        q_ref[:, s * nbk:(s + 1) * nbk] = bu[:, nbk:]

    xa_ref[hist:hist + m, :] = proj(0, c0)

    if tc == 1:
        def ssm_step(off, rb):
            rows = pl.ds(off, rb)
            ar, ai = abre_ref[...], abim_ref[...]
            hr, hi = sre0_ref[rows, :], sim0_ref[rows, :]
            nr = ar * hr - ai * hi + p_ref[rows, :]
            ni = ar * hi + ai * hr + q_ref[rows, :]
            p_ref[rows, :] = nr
            q_ref[rows, :] = ni
            sre_ref[rows, :] = nr
            sim_ref[rows, :] = ni

        for off in range(0, m, SUBLANES):
            ssm_step(off, SUBLANES)
    else:
        assert nb == SUBLANES
        for lb in range(N_STATE // SSM_LANE_BLOCK):
            lanes = slice(lb * SSM_LANE_BLOCK, (lb + 1) * SSM_LANE_BLOCK)
            ar = jnp.broadcast_to(abre_ref[:, lanes], (nb, SSM_LANE_BLOCK))
            ai = jnp.broadcast_to(abim_ref[:, lanes], (nb, SSM_LANE_BLOCK))
            hr, hi = sre_ref[:, lanes], sim_ref[:, lanes]
            for t in range(tc):
                rows = pl.ds(t * nb, nb)
                hr, hi = (ar * hr - ai * hi + p_ref[rows, lanes],
                          ar * hi + ai * hr + q_ref[rows, lanes])
                p_ref[rows, lanes] = hr
                q_ref[rows, lanes] = hi
            sre_ref[:, lanes] = hr
            sim_ref[:, lanes] = hi

    def conv(off, rb):
        acc = convb_ref[...] + convw_ref[0:1, :] * xa_ref[pl.ds(off, rb), :]
        for j in range(1, LRU_CONV):
            acc = acc + convw_ref[j:j + 1, :] * xa_ref[pl.ds(off + j * nb, rb), :]
        t1_ref[pl.ds(off, rb), :] = acc
        cb_ref[pl.ds(off, rb), :] = acc.astype(BF16)

    _for_row_blocks(m, conv)

    kc = D_SSM // SSM_SLABS
    for s in range(SSM_SLABS):
        hs = slice(s * nbk, (s + 1) * nbk)
        y_ref[:, s * kc:(s + 1) * kc] = (
            jnp.dot(p_ref[:, hs].astype(BF16), wcre_ref[s], preferred_element_type=F32)
            - jnp.dot(q_ref[:, hs].astype(BF16), wcim_ref[s], preferred_element_type=F32))

    def ssm_out(off, rb):
        rows = pl.ds(off, rb)
        ys = y_ref[rows, :] + ssmd_ref[...] * us_ref[rows, :]
        glb_ref[rows, :] = _gelu(ys).astype(BF16)

    _for_row_blocks(m, ssm_out)

    for n in range(LRU_BLOCKS):
        q_ref[:, 2 * n * LRU_BLOCK:(2 * n + 2) * LRU_BLOCK] = jnp.dot(
            cb_ref[:, n * LRU_BLOCK:(n + 1) * LRU_BLOCK], wri_ref[n],
            preferred_element_type=F32)
    ga_ref[...] = proj(c1, c2)
    p_ref[...] = (jnp.dot(glb_ref[...], wglu_ref[...], preferred_element_type=F32)
                  + bglu_ref[...])
    gb_ref[...] = proj(c2, D_IN)

    sp = _softplus(-lam_ref[...])
    rb = min(ROW_BLOCK, m)
    for n in range(LRU_BLOCKS):
        cols = slice(n * LRU_BLOCK, (n + 1) * LRU_BLOCK)
        rcols = slice(2 * n * LRU_BLOCK, (2 * n + 1) * LRU_BLOCK)
        icols = slice((2 * n + 1) * LRU_BLOCK, (2 * n + 2) * LRU_BLOCK)
        h = None if tc == 1 else h_ref[:, cols]
        for off in range(0, m, rb):
            rows = pl.ds(off, rb)
            r = _sigmoid(q_ref[rows, rcols] + br_ref[:, cols])
            i = _sigmoid(q_ref[rows, icols] + bi_ref[:, cols])
            log_a = (-LRU_C * r) * sp[:, cols]
            a = jnp.exp(log_a)
            mult = jnp.sqrt(-jnp.tanh(log_a) * (a * a + 1.0))
            bx = mult * (i * t1_ref[rows, cols])
            if tc == 1:
                hn = a * h0_ref[rows, cols] + bx
                ya_ref[rows, cols] = hn
                h_ref[rows, cols] = hn
            else:
                for k in range(0, rb, nb):
                    h = a[k:k + nb, :] * h + bx[k:k + nb, :]
                    ya_ref[pl.ds(off + k, nb), cols] = h
        if tc != 1:
            h_ref[:, cols] = h

    def merge(off, rb):
        rows = pl.ds(off, rb)
        yb = p_ref[rows, 0:D_MODEL] * _sigmoid(p_ref[rows, D_MODEL:2 * D_MODEL])
        mix = _sigmoid(ga_ref[rows, :]) * ya_ref[rows, :] + _sigmoid(gb_ref[rows, :]) * yb
        cb_ref[rows, :] = mix.astype(BF16)

    _for_row_blocks(m, merge)

    half = m // 2
    for h0 in (0, half):
        hrows = pl.ds(h0, half)
        t1_ref[hrows, :] = (jnp.dot(cb_ref[hrows, :], wout_ref[...], preferred_element_type=F32)
                            + bout_ref[...])

        def ln_out(off, rb, h0=h0):
            rows = pl.ds(h0 + off, rb)
            o_ref[rows, :] = _layer_norm(ALPHA * o_ref[rows, :] + t1_ref[rows, :],
                                         ln1g_ref[...], ln1b_ref[...])

        _for_row_blocks(half, ln_out)

    tail = xa_ref[m:m + hist, :]
    xa_ref[0:hist, :] = tail
    conv_ref[...] = tail


def _whole(shape):
    return pl.BlockSpec(memory_space=pltpu.VMEM)


def _mixer(x, conv0, h0, sre0, sim0, params, *, nb, tc):
    m = nb * tc
    row_block = pl.BlockSpec((m, D_MODEL), lambda i: (i, 0))
    if x.ndim == 3:
        assert x.shape[0] == nb and x.shape[1] % tc == 0
        m_total = nb * x.shape[1]
        x_spec = pl.BlockSpec((nb, tc, D_MODEL), lambda i: (0, i, 0))
    else:
        m_total = x.shape[0]
        x_spec = row_block
    assert m_total % m == 0
    hist = (LRU_CONV - 1) * nb
    in_specs = [x_spec] + [_whole(None)] * (4 + len(params))
    out_shape = (jax.ShapeDtypeStruct((m_total, D_MODEL), F32),
                 jax.ShapeDtypeStruct((hist, D_LRU), F32),
                 jax.ShapeDtypeStruct((nb, D_LRU), F32),
                 jax.ShapeDtypeStruct((nb, N_STATE), F32),
                 jax.ShapeDtypeStruct((nb, N_STATE), F32))
    out_specs = (row_block,
                 pl.BlockSpec((hist, D_LRU), lambda i: (0, 0)),
                 pl.BlockSpec((nb, D_LRU), lambda i: (0, 0)),
                 pl.BlockSpec((nb, N_STATE), lambda i: (0, 0)),
                 pl.BlockSpec((nb, N_STATE), lambda i: (0, 0)))
    scratch = [
        pltpu.VMEM((m, D_MODEL), BF16),
        pltpu.VMEM((hist + m, D_LRU), F32),
        pltpu.VMEM((m, D_SSM), F32),
        pltpu.VMEM((m, D_MODEL), F32),
        pltpu.VMEM((m, D_MODEL), F32),
        pltpu.VMEM((m, D_MODEL), F32),
        pltpu.VMEM((m, D_MODEL), BF16),
        pltpu.VMEM((m, N_STATE), F32),
        pltpu.VMEM((m, N_STATE), F32),
        pltpu.VMEM((m, D_LRU), F32),
        pltpu.VMEM((m, D_SSM), F32),
        pltpu.VMEM((m, D_SSM), BF16),
    ]
    return pl.pallas_call(
        functools.partial(_mixer_kernel, nb, tc),
        grid=(m_total // m,),
        in_specs=in_specs,
        out_specs=out_specs,
        out_shape=out_shape,
        scratch_shapes=scratch,
        compiler_params=pltpu.CompilerParams(
            dimension_semantics=("arbitrary",), vmem_limit_bytes=VMEM_LIMIT_BYTES),
        name=f"mixer_nb{nb}_tc{tc}",
    )(x, conv0, h0, sre0, sim0, *params)


def _ffn_kernel(nb, tc,
                x_ref, g0_ref, wup_ref, bup_ref, convw_ref, convb_ref, wdown_ref, bdown_ref,
                ln2g_ref, ln2b_ref,
                o_ref, gnew_ref,
                xb_ref, g_ref, v_ref, hb_ref, t_ref):
    m = nb * tc
    hist = (FFN_CONV - 1) * nb
    step = pl.program_id(0)
    rb = min(ROW_BLOCK, m)
    row_blocks = [pl.ds(off, rb) for off in range(0, m, rb)]

    @pl.when(step == 0)
    def _():
        g_ref[0:hist, :] = g0_ref[...]

    for rows in row_blocks:
        xb_ref[rows, :] = x_ref[rows, :].astype(BF16)

    for s in range(D_FF // FFN_SLAB):
        cols = slice(s * FFN_SLAB, (s + 1) * FFN_SLAB)
        vcols = slice(D_FF + s * FFN_SLAB, D_FF + (s + 1) * FFN_SLAB)
        g_ref[hist:hist + m, cols] = (
            jnp.dot(xb_ref[...], wup_ref[:, cols], preferred_element_type=F32)
            + bup_ref[:, cols])
        v_ref[:, cols] = (
            jnp.dot(xb_ref[...], wup_ref[:, vcols], preferred_element_type=F32)
            + bup_ref[:, vcols])
        for rows in row_blocks:
            acc = convb_ref[:, cols] + convw_ref[0:1, cols] * g_ref[rows, cols]
            for j in range(1, FFN_CONV):
                shifted = pl.ds(rows.start + j * nb, rb)
                acc = acc + convw_ref[j:j + 1, cols] * g_ref[shifted, cols]
            hb_ref[rows, cols] = (_gelu(acc) * v_ref[rows, cols]).astype(BF16)
        part = jnp.dot(hb_ref[:, cols], wdown_ref[cols, :], preferred_element_type=F32)
        if s == 0:
            t_ref[...] = part + bdown_ref[...]
        else:
            t_ref[...] += part

    for rows in row_blocks:
        out = _layer_norm(ALPHA * x_ref[rows, :] + t_ref[rows, :], ln2g_ref[...], ln2b_ref[...])
        if len(o_ref.shape) == 3:
            for k in range(rb // nb):
                o_ref[:, rows.start // nb + k, :] = out[k * nb:(k + 1) * nb, :]
        else:
            o_ref[rows, :] = out

    tail = g_ref[m:m + hist, :]
    g_ref[0:hist, :] = tail
    gnew_ref[...] = tail


def _ffn(x, g0, params, *, nb, tc, batch_major_out=False):
    m_total = x.shape[0]
    m = nb * tc
    assert m_total % m == 0
    hist = (FFN_CONV - 1) * nb
    row_block = pl.BlockSpec((m, D_MODEL), lambda i: (i, 0))
    in_specs = [row_block] + [_whole(None)] * (1 + len(params))
    if batch_major_out:
        y_shape = jax.ShapeDtypeStruct((nb, m_total // nb, D_MODEL), F32)
        y_spec = pl.BlockSpec((nb, tc, D_MODEL), lambda i: (0, i, 0))
    else:
        y_shape = jax.ShapeDtypeStruct((m_total, D_MODEL), F32)
        y_spec = row_block
    out_shape = (y_shape, jax.ShapeDtypeStruct((hist, D_FF), F32))
    out_specs = (y_spec, pl.BlockSpec((hist, D_FF), lambda i: (0, 0)))
    scratch = [
        pltpu.VMEM((m, D_MODEL), BF16),
        pltpu.VMEM((hist + m, D_FF), F32),
        pltpu.VMEM((m, D_FF), F32),
        pltpu.VMEM((m, D_FF), BF16),
        pltpu.VMEM((m, D_MODEL), F32),
    ]
    return pl.pallas_call(
        functools.partial(_ffn_kernel, nb, tc),
        grid=(m_total // m,),
        in_specs=in_specs,
        out_specs=out_specs,
        out_shape=out_shape,
        scratch_shapes=scratch,
        compiler_params=pltpu.CompilerParams(
            dimension_semantics=("arbitrary",), vmem_limit_bytes=VMEM_LIMIT_BYTES),
        name=f"ffn_nb{nb}_tc{tc}",
    )(x, g0, *params)


def _block_diag_mask(rows_per_group, cols_per_group):
    r = jnp.arange(SSM_GROUPS * rows_per_group)[:, None] // rows_per_group
    c = jnp.arange(SSM_GROUPS * cols_per_group)[None, :] // cols_per_group
    return (r == c).astype(F32)


def _ssm_matrices(bb_re_t, bb_im_t, c_re, c_im):
    mask_b = _block_diag_mask(SSM_GROUP, SSM_STATE)
    wb_re = jnp.tile(bb_re_t, (SSM_GROUPS, 1)) * mask_b
    wb_im = jnp.tile(bb_im_t, (SSM_GROUPS, 1)) * mask_b
    kb = D_SSM // SSM_SLABS
    nbk = N_STATE // SSM_SLABS
    wb = jnp.stack([
        jnp.concatenate([wb_re[s * kb:(s + 1) * kb, s * nbk:(s + 1) * nbk],
                         wb_im[s * kb:(s + 1) * kb, s * nbk:(s + 1) * nbk]], axis=1)
        for s in range(SSM_SLABS)]).astype(BF16)

    mask_c = _block_diag_mask(SSM_STATE, SSM_GROUP)

    def c_dense(c):
        ct = c.transpose(0, 2, 1).reshape(N_STATE, SSM_GROUP)
        return jnp.tile(ct, (1, SSM_GROUPS)) * mask_c

    def c_slabs(c):
        cd = c_dense(c)
        return jnp.stack([cd[s * nbk:(s + 1) * nbk, s * kb:(s + 1) * kb]
                          for s in range(SSM_SLABS)]).astype(BF16)

    return wb, c_slabs(c_re), c_slabs(c_im)


def _to_time_major(state):
    nb, k, c = state.shape
    return state.transpose(1, 0, 2).reshape(k * nb, c)


def _from_time_major(state, nb):
    k = state.shape[0] // nb
    return state.reshape(k, nb, state.shape[1]).transpose(1, 0, 2)


def kernel(x_prompt, x_sample, cache_conv_lru, state_lru, state_ssm_re, state_ssm_im, cache_conv_ffn,
           meta_tokens, emb_ln_g, emb_ln_b, w_in, b_in, conv_lru_w, conv_lru_b,
           lru_w_r, lru_b_r, lru_w_i, lru_b_i, lru_lambda,
           ssm_lambda_re, ssm_lambda_im, ssm_log_dt, ssm_b_re, ssm_b_im, ssm_c_re, ssm_c_im, ssm_d,
           w_glu, b_glu, w_out, b_out, ln1_g, ln1_b,
           w_up, b_up, ffn_conv_w, ffn_conv_b, w_down, b_down, ln2_g, ln2_b):
    assert w_in.shape[0] == DEPTH
    nbp, seq, _ = x_prompt.shape
    nbs = x_sample.shape[0]
    row = lambda a: a.reshape(1, -1).astype(F32)

    ab_re, ab_im, bb_re_t, bb_im_t = _ssm_prep(ssm_lambda_re[0], ssm_lambda_im[0], ssm_log_dt[0],
                                                ssm_b_re[0], ssm_b_im[0])
    wb, wc_re, wc_im = _ssm_matrices(bb_re_t, bb_im_t, ssm_c_re[0], ssm_c_im[0])
    wri = jnp.concatenate([lru_w_r[0], lru_w_i[0]], axis=-1).astype(BF16)

    mixer_params = (row(emb_ln_g), row(emb_ln_b), w_in[0].astype(BF16), row(b_in[0]),
                    conv_lru_w[0], row(conv_lru_b[0]),
                    wri, row(lru_b_r[0]), row(lru_b_i[0]), row(lru_lambda[0]),
                    ab_re, ab_im, wb, wc_re, wc_im, row(ssm_d[0]),
                    w_glu[0].astype(BF16), row(b_glu[0]), w_out[0].astype(BF16), row(b_out[0]),
                    row(ln1_g[0]), row(ln1_b[0]))
    ffn_params = (w_up[0].astype(BF16), row(b_up[0]), ffn_conv_w[0], row(ffn_conv_b[0]),
                  w_down[0].astype(BF16), row(b_down[0]), row(ln2_g[0]), row(ln2_b[0]))

    def layer(x, conv0, h0, sre0, sim0, g0, nb, tc):
        x1, conv, h, sre, sim = _mixer(x, conv0, h0, sre0, sim0, mixer_params, nb=nb, tc=tc)
        x2, g = _ffn(x1, g0, ffn_params, nb=nb, tc=tc, batch_major_out=(x.ndim == 3))
        return x2, (conv, h, sre, sim, g)

    zeros = lambda r, c: jnp.zeros((r, c), F32)
    meta = jnp.broadcast_to(meta_tokens[:, None, :].astype(F32), (N_META, nbp, D_MODEL))
    _, st = layer(meta.reshape(N_META * nbp, D_MODEL),
                  zeros((LRU_CONV - 1) * nbp, D_LRU), zeros(nbp, D_LRU),
                  zeros(nbp, N_STATE), zeros(nbp, N_STATE), zeros((FFN_CONV - 1) * nbp, D_FF),
                  nbp, N_META)
    y_prompt, st_p = layer(x_prompt, *st, nbp, PROMPT_STEPS_PER_CHUNK)

    ys, st_s = layer(x_sample.reshape(nbs, D_MODEL),
                     _to_time_major(cache_conv_lru[0]), state_lru[0],
                     state_ssm_re[0].reshape(nbs, N_STATE), state_ssm_im[0].reshape(nbs, N_STATE),
                     _to_time_major(cache_conv_ffn[0]), nbs, 1)
    y_sample = ys.reshape(nbs, 1, D_MODEL)

    def states(st, nb):
        conv, h, sre, sim, g = st
        return (_from_time_major(conv, nb)[None], h[None],
                sre.reshape(1, nb, SSM_GROUPS, SSM_STATE), sim.reshape(1, nb, SSM_GROUPS, SSM_STATE),
                _from_time_major(g, nb)[None])

    return (y_prompt, y_sample) + states(st_p, nbp) + states(st_s, nbs)
```

```python
import functools
import math

import jax
import jax.numpy as jnp
from jax import lax
from jax.experimental import pallas as pl
from jax.experimental.pallas import tpu as pltpu

D_MODEL = 1024
N_META = 16
D_LRU = D_MODEL
LRU_BLOCKS = 8
LRU_BLOCK = D_LRU // LRU_BLOCKS
LRU_CONV = 4
LRU_C = 8.0
D_SSM = D_MODEL // 2
SSM_GROUP = 16
SSM_GROUPS = D_SSM // SSM_GROUP
SSM_STATE = 64
N_STATE = SSM_GROUPS * SSM_STATE
D_FF = 3 * D_MODEL
FFN_CONV = 3
D_IN = D_LRU + D_SSM + 2 * D_MODEL
DEPTH = 1
ALPHA = (2.0 * DEPTH) ** 0.25
LN_EPS = 1e-5
GELU_C = math.sqrt(2.0 / math.pi)

V7X_VMEM_BYTES = 64 * 1024 * 1024
VMEM_LIMIT_BYTES = V7X_VMEM_BYTES - 8 * 1024 * 1024
SUBLANES = 8
ROW_BLOCK = 64
SSM_SLABS = 2
SSM_LANE_BLOCK = 512
PROMPT_STEPS_PER_CHUNK = 64
FFN_SLAB = 1024

BF16 = jnp.bfloat16
F32 = jnp.float32


def _layer_norm(x, g, b):
    mu = jnp.mean(x, axis=-1, keepdims=True)
    xc = x - mu
    var = jnp.mean(xc * xc, axis=-1, keepdims=True)
    return xc * lax.rsqrt(var + LN_EPS) * g + b


def _gelu(x):
    return 0.5 * x * (1.0 + jnp.tanh(GELU_C * (x + 0.044715 * (x * x * x))))


def _sigmoid(x):
    return 0.5 * jnp.tanh(0.5 * x) + 0.5


def _softplus(x):
    return jnp.maximum(x, 0.0) + jnp.log1p(jnp.exp(-jnp.abs(x)))


def _for_row_blocks(n_rows, fn):
    rb = min(ROW_BLOCK, n_rows)
    assert n_rows % rb == 0
    for off in range(0, n_rows, rb):
        fn(off, rb)


def _ssm_prep_kernel(lre_ref, lim_ref, logdt_ref, bre_ref, bim_ref,
                     abre_ref, abim_ref, bbre_ref, bbim_ref):
    lre = lre_ref[...]
    lim = lim_ref[...]
    dt = jnp.exp(logdt_ref[...])
    mag = jnp.exp(lre * dt)
    ab_re = mag * jnp.cos(lim * dt)
    ab_im = mag * jnp.sin(lim * dt)
    den = lre * lre + lim * lim
    nr = ab_re - 1.0
    ni = ab_im
    f_re = (nr * lre + ni * lim) / den
    f_im = (ni * lre - nr * lim) / den
    br = bre_ref[...]
    bi = bim_ref[...]
    abre_ref[...] = ab_re
    abim_ref[...] = ab_im
    bbre_ref[...] = f_re * br - f_im * bi
    bbim_ref[...] = f_re * bi + f_im * br


def _ssm_prep(lam_re, lam_im, log_dt, b_re, b_im):
    row = lambda a: a.reshape(1, N_STATE)
    b_t = lambda a: a.transpose(2, 0, 1).reshape(SSM_GROUP, N_STATE)
    logdt = jnp.repeat(log_dt, SSM_STATE).reshape(1, N_STATE)
    out_shape = (jax.ShapeDtypeStruct((1, N_STATE), F32),) * 2 + (
        jax.ShapeDtypeStruct((SSM_GROUP, N_STATE), F32),) * 2
    return pl.pallas_call(_ssm_prep_kernel, out_shape=out_shape, name="ssm_prep")(
        row(lam_re), row(lam_im), logdt, b_t(b_re), b_t(b_im))


def _mixer_kernel(nb, tc,
                  x_ref, conv0_ref, h0_ref, sre0_ref, sim0_ref,
                  embg_ref, embb_ref, win_ref, bin_ref, convw_ref, convb_ref,
                  wri_ref, br_ref, bi_ref, lam_ref,
                  abre_ref, abim_ref, wb_ref, wcre_ref, wcim_ref, ssmd_ref,
                  wglu_ref, bglu_ref, wout_ref, bout_ref, ln1g_ref, ln1b_ref,
                  o_ref, conv_ref, h_ref, sre_ref, sim_ref,
                  zb_ref, xa_ref, us_ref, ga_ref, gb_ref, t1_ref, cb_ref, p_ref, q_ref,
                  ya_ref, y_ref, glb_ref):
    m = nb * tc
    hist = (LRU_CONV - 1) * nb
    step = pl.program_id(0)

    @pl.when(step == 0)
    def _():
        xa_ref[0:hist, :] = conv0_ref[...]
        h_ref[...] = h0_ref[...]
        sre_ref[...] = sre0_ref[...]
        sim_ref[...] = sim0_ref[...]

    if len(x_ref.shape) == 3:
        for t in range(tc):
            o_ref[pl.ds(t * nb, nb), :] = x_ref[:, t, :]
        src_ref = o_ref
    else:
        src_ref = x_ref

    def ln_in(off, rb):
        z = _layer_norm(src_ref[pl.ds(off, rb), :], embg_ref[...], embb_ref[...])
        o_ref[pl.ds(off, rb), :] = z
        zb_ref[pl.ds(off, rb), :] = z.astype(BF16)

    _for_row_blocks(m, ln_in)

    def proj(lo, hi):
        return (jnp.dot(zb_ref[...], win_ref[:, lo:hi], preferred_element_type=F32)
                + bin_ref[:, lo:hi])

    c0, c1, c2 = D_LRU, D_LRU + D_SSM, D_LRU + D_SSM + D_MODEL
    us_ref[...] = proj(c0, c1)

    kb = D_SSM // SSM_SLABS
    nbk = N_STATE // SSM_SLABS
    for s in range(SSM_SLABS):
        bu = jnp.dot(us_ref[:, s * kb:(s + 1) * kb].astype(BF16), wb_ref[s],
                     preferred_element_type=F32)
        p_ref[:, s * nbk:(s + 1) * nbk] = bu[:, :nbk]
        q_ref[:, s * nbk:(s + 1) * nbk] = bu[:, nbk:]

    xa_ref[hist:hist + m, :] = proj(0, c0)

    if tc == 1:
        def ssm_step(off, rb):
            rows = pl.ds(off, rb)
            ar, ai = abre_ref[...], abim_ref[...]
            hr, hi = sre0_ref[rows, :], sim0_ref[rows, :]
            nr = ar * hr - ai * hi + p_ref[rows, :]
            ni = ar * hi + ai * hr + q_ref[rows, :]
            p_ref[rows, :] = nr
            q_ref[rows, :] = ni
            sre_ref[rows, :] = nr
            sim_ref[rows, :] = ni

        for off in range(0, m, SUBLANES):
            ssm_step(off, SUBLANES)
    else:
        assert nb == SUBLANES
        for lb in range(N_STATE // SSM_LANE_BLOCK):
            lanes = slice(lb * SSM_LANE_BLOCK, (lb + 1) * SSM_LANE_BLOCK)
            ar = jnp.broadcast_to(abre_ref[:, lanes], (nb, SSM_LANE_BLOCK))
            ai = jnp.broadcast_to(abim_ref[:, lanes], (nb, SSM_LANE_BLOCK))
            hr, hi = sre_ref[:, lanes], sim_ref[:, lanes]
            for t in range(tc):
                rows = pl.ds(t * nb, nb)
                hr, hi = (ar * hr - ai * hi + p_ref[rows, lanes],
                          ar * hi + ai * hr + q_ref[rows, lanes])
                p_ref[rows, lanes] = hr
                q_ref[rows, lanes] = hi
            sre_ref[:, lanes] = hr
            sim_ref[:, lanes] = hi

    def conv(off, rb):
        acc = convb_ref[...] + convw_ref[0:1, :] * xa_ref[pl.ds(off, rb), :]
        for j in range(1, LRU_CONV):
            acc = acc + convw_ref[j:j + 1, :] * xa_ref[pl.ds(off + j * nb, rb), :]
        t1_ref[pl.ds(off, rb), :] = acc
        cb_ref[pl.ds(off, rb), :] = acc.astype(BF16)

    _for_row_blocks(m, conv)

    kc = D_SSM // SSM_SLABS
    for s in range(SSM_SLABS):
        hs = slice(s * nbk, (s + 1) * nbk)
        y_ref[:, s * kc:(s + 1) * kc] = (
            jnp.dot(p_ref[:, hs].astype(BF16), wcre_ref[s], preferred_element_type=F32)
            - jnp.dot(q_ref[:, hs].astype(BF16), wcim_ref[s], preferred_element_type=F32))

    def ssm_out(off, rb):
        rows = pl.ds(off, rb)
        ys = y_ref[rows, :] + ssmd_ref[...] * us_ref[rows, :]
        glb_ref[rows, :] = _gelu(ys).astype(BF16)

    _for_row_blocks(m, ssm_out)

    for n in range(LRU_BLOCKS):
        q_ref[:, 2 * n * LRU_BLOCK:(2 * n + 2) * LRU_BLOCK] = jnp.dot(
            cb_ref[:, n * LRU_BLOCK:(n + 1) * LRU_BLOCK], wri_ref[n],
            preferred_element_type=F32)
    ga_ref[...] = proj(c1, c2)
    p_ref[...] = (jnp.dot(glb_ref[...], wglu_ref[...], preferred_element_type=F32)
                  + bglu_ref[...])
    gb_ref[...] = proj(c2, D_IN)

    sp = _softplus(-lam_ref[...])
    rb = min(ROW_BLOCK, m)
    for n in range(LRU_BLOCKS):
        cols = slice(n * LRU_BLOCK, (n + 1) * LRU_BLOCK)
        rcols = slice(2 * n * LRU_BLOCK, (2 * n + 1) * LRU_BLOCK)
        icols = slice((2 * n + 1) * LRU_BLOCK, (2 * n + 2) * LRU_BLOCK)
        h = None if tc == 1 else h_ref[:, cols]
        for off in range(0, m, rb):
            rows = pl.ds(off, rb)
            r = _sigmoid(q_ref[rows, rcols] + br_ref[:, cols])
            i = _sigmoid(q_ref[rows, icols] + bi_ref[:, cols])
            log_a = (-LRU_C * r) * sp[:, cols]
            a = jnp.exp(log_a)
            mult = jnp.sqrt(-jnp.tanh(log_a) * (a * a + 1.0))
            bx = mult * (i * t1_ref[rows, cols])
            if tc == 1:
                hn = a * h0_ref[rows, cols] + bx
                ya_ref[rows, cols] = hn
                h_ref[rows, cols] = hn
            else:
                for k in range(0, rb, nb):
                    h = a[k:k + nb, :] * h + bx[k:k + nb, :]
                    ya_ref[pl.ds(off + k, nb), cols] = h
        if tc != 1:
            h_ref[:, cols] = h

    def merge(off, rb):
        rows = pl.ds(off, rb)
        yb = p_ref[rows, 0:D_MODEL] * _sigmoid(p_ref[rows, D_MODEL:2 * D_MODEL])
        mix = _sigmoid(ga_ref[rows, :]) * ya_ref[rows, :] + _sigmoid(gb_ref[rows, :]) * yb
        cb_ref[rows, :] = mix.astype(BF16)

    _for_row_blocks(m, merge)

    half = m // 2
    for h0 in (0, half):
        hrows = pl.ds(h0, half)
        t1_ref[hrows, :] = (jnp.dot(cb_ref[hrows, :], wout_ref[...], preferred_element_type=F32)
                            + bout_ref[...])

        def ln_out(off, rb, h0=h0):
            rows = pl.ds(h0 + off, rb)
            o_ref[rows, :] = _layer_norm(ALPHA * o_ref[rows, :] + t1_ref[rows, :],
                                         ln1g_ref[...], ln1b_ref[...])

        _for_row_blocks(half, ln_out)

    tail = xa_ref[m:m + hist, :]
    xa_ref[0:hist, :] = tail
    conv_ref[...] = tail


def _whole(shape):
    return pl.BlockSpec(memory_space=pltpu.VMEM)


def _mixer(x, conv0, h0, sre0, sim0, params, *, nb, tc):
    m = nb * tc
    row_block = pl.BlockSpec((m, D_MODEL), lambda i: (i, 0))
    if x.ndim == 3:
        assert x.shape[0] == nb and x.shape[1] % tc == 0
        m_total = nb * x.shape[1]
        x_spec = pl.BlockSpec((nb, tc, D_MODEL), lambda i: (0, i, 0))
    else:
        m_total = x.shape[0]
        x_spec = row_block
    assert m_total % m == 0
    hist = (LRU_CONV - 1) * nb
    in_specs = [x_spec] + [_whole(None)] * (4 + len(params))
    out_shape = (jax.ShapeDtypeStruct((m_total, D_MODEL), F32),
                 jax.ShapeDtypeStruct((hist, D_LRU), F32),
                 jax.ShapeDtypeStruct((nb, D_LRU), F32),
                 jax.ShapeDtypeStruct((nb, N_STATE), F32),
                 jax.ShapeDtypeStruct((nb, N_STATE), F32))
    out_specs = (row_block,
                 pl.BlockSpec((hist, D_LRU), lambda i: (0, 0)),
                 pl.BlockSpec((nb, D_LRU), lambda i: (0, 0)),
                 pl.BlockSpec((nb, N_STATE), lambda i: (0, 0)),
                 pl.BlockSpec((nb, N_STATE), lambda i: (0, 0)))
    scratch = [
        pltpu.VMEM((m, D_MODEL), BF16),
        pltpu.VMEM((hist + m, D_LRU), F32),
        pltpu.VMEM((m, D_SSM), F32),
        pltpu.VMEM((m, D_MODEL), F32),
        pltpu.VMEM((m, D_MODEL), F32),
        pltpu.VMEM((m, D_MODEL), F32),
        pltpu.VMEM((m, D_MODEL), BF16),
        pltpu.VMEM((m, N_STATE), F32),
        pltpu.VMEM((m, N_STATE), F32),
        pltpu.VMEM((m, D_LRU), F32),
        pltpu.VMEM((m, D_SSM), F32),
        pltpu.VMEM((m, D_SSM), BF16),
    ]
    return pl.pallas_call(
        functools.partial(_mixer_kernel, nb, tc),
        grid=(m_total // m,),
        in_specs=in_specs,
        out_specs=out_specs,
        out_shape=out_shape,
        scratch_shapes=scratch,
        compiler_params=pltpu.CompilerParams(
            dimension_semantics=("arbitrary",), vmem_limit_bytes=VMEM_LIMIT_BYTES),
        name=f"mixer_nb{nb}_tc{tc}",
    )(x, conv0, h0, sre0, sim0, *params)


def _ffn_kernel(nb, tc,
                x_ref, g0_ref, wup_ref, bup_ref, convw_ref, convb_ref, wdown_ref, bdown_ref,
                ln2g_ref, ln2b_ref,
                o_ref, gnew_ref,
                xb_ref, g_ref, v_ref, hb_ref, t_ref):
    m = nb * tc
    hist = (FFN_CONV - 1) * nb
    step = pl.program_id(0)
    rb = min(ROW_BLOCK, m)
    row_blocks = [pl.ds(off, rb) for off in range(0, m, rb)]

    @pl.when(step == 0)
    def _():
        g_ref[0:hist, :] = g0_ref[...]

    for rows in row_blocks:
        xb_ref[rows, :] = x_ref[rows, :].astype(BF16)

    for s in range(D_FF // FFN_SLAB):
        cols = slice(s * FFN_SLAB, (s + 1) * FFN_SLAB)
        vcols = slice(D_FF + s * FFN_SLAB, D_FF + (s + 1) * FFN_SLAB)
        g_ref[hist:hist + m, cols] = (
            jnp.dot(xb_ref[...], wup_ref[:, cols], preferred_element_type=F32)
            + bup_ref[:, cols])
        v_ref[:, cols] = (
            jnp.dot(xb_ref[...], wup_ref[:, vcols], preferred_element_type=F32)
            + bup_ref[:, vcols])
        for rows in row_blocks:
            acc = convb_ref[:, cols] + convw_ref[0:1, cols] * g_ref[rows, cols]
            for j in range(1, FFN_CONV):
                shifted = pl.ds(rows.start + j * nb, rb)
                acc = acc + convw_ref[j:j + 1, cols] * g_ref[shifted, cols]
            hb_ref[rows, cols] = (_gelu(acc) * v_ref[rows, cols]).astype(BF16)
        part = jnp.dot(hb_ref[:, cols], wdown_ref[cols, :], preferred_element_type=F32)
        if s == 0:
            t_ref[...] = part + bdown_ref[...]
        else:
            t_ref[...] += part

    for rows in row_blocks:
        out = _layer_norm(ALPHA * x_ref[rows, :] + t_ref[rows, :], ln2g_ref[...], ln2b_ref[...])
        if len(o_ref.shape) == 3:
            for k in range(rb // nb):
                o_ref[:, rows.start // nb + k, :] = out[k * nb:(k + 1) * nb, :]
        else:
            o_ref[rows, :] = out

    tail = g_ref[m:m + hist, :]
    g_ref[0:hist, :] = tail
    gnew_ref[...] = tail


def _ffn(x, g0, params, *, nb, tc, batch_major_out=False):
    m_total = x.shape[0]
    m = nb * tc
    assert m_total % m == 0
    hist = (FFN_CONV - 1) * nb
    row_block = pl.BlockSpec((m, D_MODEL), lambda i: (i, 0))
    in_specs = [row_block] + [_whole(None)] * (1 + len(params))
    if batch_major_out:
        y_shape = jax.ShapeDtypeStruct((nb, m_total // nb, D_MODEL), F32)
        y_spec = pl.BlockSpec((nb, tc, D_MODEL), lambda i: (0, i, 0))
    else:
        y_shape = jax.ShapeDtypeStruct((m_total, D_MODEL), F32)
        y_spec = row_block
    out_shape = (y_shape, jax.ShapeDtypeStruct((hist, D_FF), F32))
    out_specs = (y_spec, pl.BlockSpec((hist, D_FF), lambda i: (0, 0)))
    scratch = [
        pltpu.VMEM((m, D_MODEL), BF16),
        pltpu.VMEM((hist + m, D_FF), F32),
        pltpu.VMEM((m, D_FF), F32),
        pltpu.VMEM((m, D_FF), BF16),
        pltpu.VMEM((m, D_MODEL), F32),
    ]
    return pl.pallas_call(
        functools.partial(_ffn_kernel, nb, tc),
        grid=(m_total // m,),
        in_specs=in_specs,
        out_specs=out_specs,
        out_shape=out_shape,
        scratch_shapes=scratch,
        compiler_params=pltpu.CompilerParams(
            dimension_semantics=("arbitrary",), vmem_limit_bytes=VMEM_LIMIT_BYTES),
        name=f"ffn_nb{nb}_tc{tc}",
    )(x, g0, *params)


def _block_diag_mask(rows_per_group, cols_per_group):
    r = jnp.arange(SSM_GROUPS * rows_per_group)[:, None] // rows_per_group
    c = jnp.arange(SSM_GROUPS * cols_per_group)[None, :] // cols_per_group
    return (r == c).astype(F32)


def _ssm_matrices(bb_re_t, bb_im_t, c_re, c_im):
    mask_b = _block_diag_mask(SSM_GROUP, SSM_STATE)
    wb_re = jnp.tile(bb_re_t, (SSM_GROUPS, 1)) * mask_b
    wb_im = jnp.tile(bb_im_t, (SSM_GROUPS, 1)) * mask_b
    kb = D_SSM // SSM_SLABS
    nbk = N_STATE // SSM_SLABS
    wb = jnp.stack([
        jnp.concatenate([wb_re[s * kb:(s + 1) * kb, s * nbk:(s + 1) * nbk],
                         wb_im[s * kb:(s + 1) * kb, s * nbk:(s + 1) * nbk]], axis=1)
        for s in range(SSM_SLABS)]).astype(BF16)

    mask_c = _block_diag_mask(SSM_STATE, SSM_GROUP)

    def c_dense(c):
        ct = c.transpose(0, 2, 1).reshape(N_STATE, SSM_GROUP)
        return jnp.tile(ct, (1, SSM_GROUPS)) * mask_c

    def c_slabs(c):
        cd = c_dense(c)
        return jnp.stack([cd[s * nbk:(s + 1) * nbk, s * kb:(s + 1) * kb]
                          for s in range(SSM_SLABS)]).astype(BF16)

    return wb, c_slabs(c_re), c_slabs(c_im)


def _to_time_major(state):
    nb, k, c = state.shape
    return state.transpose(1, 0, 2).reshape(k * nb, c)


def _from_time_major(state, nb):
    k = state.shape[0] // nb
    return state.reshape(k, nb, state.shape[1]).transpose(1, 0, 2)


def kernel(x_prompt, x_sample, cache_conv_lru, state_lru, state_ssm_re, state_ssm_im, cache_conv_ffn,
           meta_tokens, emb_ln_g, emb_ln_b, w_in, b_in, conv_lru_w, conv_lru_b,
           lru_w_r, lru_b_r, lru_w_i, lru_b_i, lru_lambda,
           ssm_lambda_re, ssm_lambda_im, ssm_log_dt, ssm_b_re, ssm_b_im, ssm_c_re, ssm_c_im, ssm_d,
           w_glu, b_glu, w_out, b_out, ln1_g, ln1_b,
           w_up, b_up, ffn_conv_w, ffn_conv_b, w_down, b_down, ln2_g, ln2_b):
    assert w_in.shape[0] == DEPTH
    nbp, seq, _ = x_prompt.shape
    nbs = x_sample.shape[0]
    row = lambda a: a.reshape(1, -1).astype(F32)

    ab_re, ab_im, bb_re_t, bb_im_t = _ssm_prep(ssm_lambda_re[0], ssm_lambda_im[0], ssm_log_dt[0],
                                                ssm_b_re[0], ssm_b_im[0])
    wb, wc_re, wc_im = _ssm_matrices(bb_re_t, bb_im_t, ssm_c_re[0], ssm_c_im[0])
    wri = jnp.concatenate([lru_w_r[0], lru_w_i[0]], axis=-1).astype(BF16)

    mixer_params = (row(emb_ln_g), row(emb_ln_b), w_in[0].astype(BF16), row(b_in[0]),
                    conv_lru_w[0], row(conv_lru_b[0]),
                    wri, row(lru_b_r[0]), row(lru_b_i[0]), row(lru_lambda[0]),
                    ab_re, ab_im, wb, wc_re, wc_im, row(ssm_d[0]),
                    w_glu[0].astype(BF16), row(b_glu[0]), w_out[0].astype(BF16), row(b_out[0]),
                    row(ln1_g[0]), row(ln1_b[0]))
    ffn_params = (w_up[0].astype(BF16), row(b_up[0]), ffn_conv_w[0], row(ffn_conv_b[0]),
                  w_down[0].astype(BF16), row(b_down[0]), row(ln2_g[0]), row(ln2_b[0]))

    def layer(x, conv0, h0, sre0, sim0, g0, nb, tc):
        x1, conv, h, sre, sim = _mixer(x, conv0, h0, sre0, sim0, mixer_params, nb=nb, tc=tc)
        x2, g = _ffn(x1, g0, ffn_params, nb=nb, tc=tc, batch_major_out=(x.ndim == 3))
        return x2, (conv, h, sre, sim, g)

    zeros = lambda r, c: jnp.zeros((r, c), F32)
    meta = jnp.broadcast_to(meta_tokens[:, None, :].astype(F32), (N_META, nbp, D_MODEL))
    _, st = layer(meta.reshape(N_META * nbp, D_MODEL),
                  zeros((LRU_CONV - 1) * nbp, D_LRU), zeros(nbp, D_LRU),
                  zeros(nbp, N_STATE), zeros(nbp, N_STATE), zeros((FFN_CONV - 1) * nbp, D_FF),
                  nbp, N_META)
    y_prompt, st_p = layer(x_prompt, *st, nbp, PROMPT_STEPS_PER_CHUNK)

    ys, st_s = layer(x_sample.reshape(nbs, D_MODEL),
                     _to_time_major(cache_conv_lru[0]), state_lru[0],
                     state_ssm_re[0].reshape(nbs, N_STATE), state_ssm_im[0].reshape(nbs, N_STATE),
                     _to_time_major(cache_conv_ffn[0]), nbs, 1)
    y_sample = ys.reshape(nbs, 1, D_MODEL)

    def states(st, nb):
        conv, h, sre, sim, g = st
        return (_from_time_major(conv, nb)[None], h[None],
                sre.reshape(1, nb, SSM_GROUPS, SSM_STATE), sim.reshape(1, nb, SSM_GROUPS, SSM_STATE),
                _from_time_major(g, nb)[None])

    return (y_prompt, y_sample) + states(st_p, nbp) + states(st_s, nbs)
```

```python
import functools
import math

import jax
import jax.numpy as jnp
from jax import lax
from jax.experimental import pallas as pl
from jax.experimental.pallas import tpu as pltpu

D_MODEL = 1024
N_META = 16
D_LRU = D_MODEL
LRU_BLOCKS = 8
LRU_BLOCK = D_LRU // LRU_BLOCKS
LRU_CONV = 4
LRU_C = 8.0
D_SSM = D_MODEL // 2
SSM_GROUP = 16
SSM_GROUPS = D_SSM // SSM_GROUP
SSM_STATE = 64
N_STATE = SSM_GROUPS * SSM_STATE
D_FF = 3 * D_MODEL
FFN_CONV = 3
D_IN = D_LRU + D_SSM + 2 * D_MODEL
DEPTH = 1
ALPHA = (2.0 * DEPTH) ** 0.25
LN_EPS = 1e-5
GELU_C = math.sqrt(2.0 / math.pi)

V7X_VMEM_BYTES = 64 * 1024 * 1024
VMEM_LIMIT_BYTES = V7X_VMEM_BYTES - 8 * 1024 * 1024
SUBLANES = 8
ROW_BLOCK = 256
SSM_SLABS = 2
SSM_LANE_BLOCK = 512
PROMPT_STEPS_PER_CHUNK = 64
FFN_SLAB = 1024

BF16 = jnp.bfloat16
F32 = jnp.float32


def _layer_norm(x, g, b):
    mu = jnp.mean(x, axis=-1, keepdims=True)
    xc = x - mu
    var = jnp.mean(xc * xc, axis=-1, keepdims=True)
    return xc * lax.rsqrt(var + LN_EPS) * g + b


def _gelu(x):
    return 0.5 * x * (1.0 + jnp.tanh(GELU_C * (x + 0.044715 * (x * x * x))))


def _sigmoid(x):
    return 0.5 * jnp.tanh(0.5 * x) + 0.5


def _softplus(x):
    return jnp.maximum(x, 0.0) + jnp.log1p(jnp.exp(-jnp.abs(x)))


def _for_row_blocks(n_rows, fn):
    rb = min(ROW_BLOCK, n_rows)
    assert n_rows % rb == 0
    for off in range(0, n_rows, rb):
        fn(off, rb)


def _ssm_prep_kernel(lre_ref, lim_ref, logdt_ref, bre_ref, bim_ref,
                     abre_ref, abim_ref, bbre_ref, bbim_ref):
    lre = lre_ref[...]
    lim = lim_ref[...]
    dt = jnp.exp(logdt_ref[...])
    mag = jnp.exp(lre * dt)
    ab_re = mag * jnp.cos(lim * dt)
    ab_im = mag * jnp.sin(lim * dt)
    den = lre * lre + lim * lim
    nr = ab_re - 1.0
    ni = ab_im
    f_re = (nr * lre + ni * lim) / den
    f_im = (ni * lre - nr * lim) / den
    br = bre_ref[...]
    bi = bim_ref[...]
    abre_ref[...] = ab_re
    abim_ref[...] = ab_im
    bbre_ref[...] = f_re * br - f_im * bi
    bbim_ref[...] = f_re * bi + f_im * br


def _ssm_prep(lam_re, lam_im, log_dt, b_re, b_im):
    row = lambda a: a.reshape(1, N_STATE)
    b_t = lambda a: a.transpose(2, 0, 1).reshape(SSM_GROUP, N_STATE)
    logdt = jnp.repeat(log_dt, SSM_STATE).reshape(1, N_STATE)
    out_shape = (jax.ShapeDtypeStruct((1, N_STATE), F32),) * 2 + (
        jax.ShapeDtypeStruct((SSM_GROUP, N_STATE), F32),) * 2
    return pl.pallas_call(_ssm_prep_kernel, out_shape=out_shape, name="ssm_prep")(
        row(lam_re), row(lam_im), logdt, b_t(b_re), b_t(b_im))


def _mixer_kernel(nb, tc,
                  x_ref, conv0_ref, h0_ref, sre0_ref, sim0_ref,
                  embg_ref, embb_ref, win_ref, bin_ref, convw_ref, convb_ref,
                  wri_ref, br_ref, bi_ref, lam_ref,
                  abre_ref, abim_ref, wb_ref, wcre_ref, wcim_ref, ssmd_ref,
                  wglu_ref, bglu_ref, wout_ref, bout_ref, ln1g_ref, ln1b_ref,
                  o_ref, conv_ref, h_ref, sre_ref, sim_ref,
                  zb_ref, xa_ref, us_ref, ga_ref, gb_ref, t1_ref, cb_ref, p_ref, q_ref,
                  ya_ref, y_ref, glb_ref):
    m = nb * tc
    hist = (LRU_CONV - 1) * nb
    step = pl.program_id(0)

    @pl.when(step == 0)
    def _():
        xa_ref[0:hist, :] = conv0_ref[...]
        h_ref[...] = h0_ref[...]
        sre_ref[...] = sre0_ref[...]
        sim_ref[...] = sim0_ref[...]

    if len(x_ref.shape) == 3:
        for t in range(tc):
            o_ref[pl.ds(t * nb, nb), :] = x_ref[:, t, :]
        src_ref = o_ref
    else:
        src_ref = x_ref

    def ln_in(off, rb):
        z = _layer_norm(src_ref[pl.ds(off, rb), :], embg_ref[...], embb_ref[...])
        o_ref[pl.ds(off, rb), :] = z
        zb_ref[pl.ds(off, rb), :] = z.astype(BF16)

    _for_row_blocks(m, ln_in)

    def proj(lo, hi):
        return (jnp.dot(zb_ref[...], win_ref[:, lo:hi], preferred_element_type=F32)
                + bin_ref[:, lo:hi])

    c0, c1, c2 = D_LRU, D_LRU + D_SSM, D_LRU + D_SSM + D_MODEL
    us_ref[...] = proj(c0, c1)

    kb = D_SSM // SSM_SLABS
    nbk = N_STATE // SSM_SLABS
    for s in range(SSM_SLABS):
        bu = jnp.dot(us_ref[:, s * kb:(s + 1) * kb].astype(BF16), wb_ref[s],
                     preferred_element_type=F32)
        p_ref[:, s * nbk:(s + 1) * nbk] = bu[:, :nbk]
        q_ref[:, s * nbk:(s + 1) * nbk] = bu[:, nbk:]

    xa_ref[hist:hist + m, :] = proj(0, c0)

    if tc == 1:
        def ssm_step(off, rb):
            rows = pl.ds(off, rb)
            ar, ai = abre_ref[...], abim_ref[...]
            hr, hi = sre0_ref[rows, :], sim0_ref[rows, :]
            nr = ar * hr - ai * hi + p_ref[rows, :]
            ni = ar * hi + ai * hr + q_ref[rows, :]
            p_ref[rows, :] = nr
            q_ref[rows, :] = ni
            sre_ref[rows, :] = nr
            sim_ref[rows, :] = ni

        for off in range(0, m, SUBLANES):
            ssm_step(off, SUBLANES)
    else:
        assert nb == SUBLANES
        for lb in range(N_STATE // SSM_LANE_BLOCK):
            lanes = slice(lb * SSM_LANE_BLOCK, (lb + 1) * SSM_LANE_BLOCK)
            ar = jnp.broadcast_to(abre_ref[:, lanes], (nb, SSM_LANE_BLOCK))
            ai = jnp.broadcast_to(abim_ref[:, lanes], (nb, SSM_LANE_BLOCK))
            hr, hi = sre_ref[:, lanes], sim_ref[:, lanes]
            for t in range(tc):
                rows = pl.ds(t * nb, nb)
                hr, hi = (ar * hr - ai * hi + p_ref[rows, lanes],
                          ar * hi + ai * hr + q_ref[rows, lanes])
                p_ref[rows, lanes] = hr
                q_ref[rows, lanes] = hi
            sre_ref[:, lanes] = hr
            sim_ref[:, lanes] = hi

    def conv(off, rb):
        acc = convb_ref[...] + convw_ref[0:1, :] * xa_ref[pl.ds(off, rb), :]
        for j in range(1, LRU_CONV):
            acc = acc + convw_ref[j:j + 1, :] * xa_ref[pl.ds(off + j * nb, rb), :]
        t1_ref[pl.ds(off, rb), :] = acc
        cb_ref[pl.ds(off, rb), :] = acc.astype(BF16)

    _for_row_blocks(m, conv)

    kc = D_SSM // SSM_SLABS
    for s in range(SSM_SLABS):
        hs = slice(s * nbk, (s + 1) * nbk)
        y_ref[:, s * kc:(s + 1) * kc] = (
            jnp.dot(p_ref[:, hs].astype(BF16), wcre_ref[s], preferred_element_type=F32)
            - jnp.dot(q_ref[:, hs].astype(BF16), wcim_ref[s], preferred_element_type=F32))

    def ssm_out(off, rb):
        rows = pl.ds(off, rb)
        ys = y_ref[rows, :] + ssmd_ref[...] * us_ref[rows, :]
        glb_ref[rows, :] = _gelu(ys).astype(BF16)

    _for_row_blocks(m, ssm_out)

    for n in range(LRU_BLOCKS):
        q_ref[:, 2 * n * LRU_BLOCK:(2 * n + 2) * LRU_BLOCK] = jnp.dot(
            cb_ref[:, n * LRU_BLOCK:(n + 1) * LRU_BLOCK], wri_ref[n],
            preferred_element_type=F32)
    ga_ref[...] = proj(c1, c2)
    p_ref[...] = (jnp.dot(glb_ref[...], wglu_ref[...], preferred_element_type=F32)
                  + bglu_ref[...])
    gb_ref[...] = proj(c2, D_IN)

    sp = _softplus(-lam_ref[...])
    rb = min(ROW_BLOCK, m)
    for n in range(LRU_BLOCKS):
        cols = slice(n * LRU_BLOCK, (n + 1) * LRU_BLOCK)
        rcols = slice(2 * n * LRU_BLOCK, (2 * n + 1) * LRU_BLOCK)
        icols = slice((2 * n + 1) * LRU_BLOCK, (2 * n + 2) * LRU_BLOCK)
        h = None if tc == 1 else h_ref[:, cols]
        for off in range(0, m, rb):
            rows = pl.ds(off, rb)
            r = _sigmoid(q_ref[rows, rcols] + br_ref[:, cols])
            i = _sigmoid(q_ref[rows, icols] + bi_ref[:, cols])
            log_a = (-LRU_C * r) * sp[:, cols]
            a = jnp.exp(log_a)
            mult = jnp.sqrt(-jnp.tanh(log_a) * (a * a + 1.0))
            bx = mult * (i * t1_ref[rows, cols])
            if tc == 1:
                hn = a * h0_ref[rows, cols] + bx
                ya_ref[rows, cols] = hn
                h_ref[rows, cols] = hn
            else:
                for k in range(0, rb, nb):
                    h = a[k:k + nb, :] * h + bx[k:k + nb, :]
                    ya_ref[pl.ds(off + k, nb), cols] = h
        if tc != 1:
            h_ref[:, cols] = h

    def merge(off, rb):
        rows = pl.ds(off, rb)
        yb = p_ref[rows, 0:D_MODEL] * _sigmoid(p_ref[rows, D_MODEL:2 * D_MODEL])
        mix = _sigmoid(ga_ref[rows, :]) * ya_ref[rows, :] + _sigmoid(gb_ref[rows, :]) * yb
        cb_ref[rows, :] = mix.astype(BF16)

    _for_row_blocks(m, merge)

    half = m // 2
    for h0 in (0, half):
        hrows = pl.ds(h0, half)
        t1_ref[hrows, :] = (jnp.dot(cb_ref[hrows, :], wout_ref[...], preferred_element_type=F32)
                            + bout_ref[...])

        def ln_out(off, rb, h0=h0):
            rows = pl.ds(h0 + off, rb)
            o_ref[rows, :] = _layer_norm(ALPHA * o_ref[rows, :] + t1_ref[rows, :],
                                         ln1g_ref[...], ln1b_ref[...])

        _for_row_blocks(half, ln_out)

    tail = xa_ref[m:m + hist, :]
    xa_ref[0:hist, :] = tail
    conv_ref[...] = tail


def _whole(shape):
    return pl.BlockSpec(memory_space=pltpu.VMEM)


def _mixer(x, conv0, h0, sre0, sim0, params, *, nb, tc):
    m = nb * tc
    row_block = pl.BlockSpec((m, D_MODEL), lambda i: (i, 0))
    if x.ndim == 3:
        assert x.shape[0] == nb and x.shape[1] % tc == 0
        m_total = nb * x.shape[1]
        x_spec = pl.BlockSpec((nb, tc, D_MODEL), lambda i: (0, i, 0))
    else:
        m_total = x.shape[0]
        x_spec = row_block
    assert m_total % m == 0
    hist = (LRU_CONV - 1) * nb
    in_specs = [x_spec] + [_whole(None)] * (4 + len(params))
    out_shape = (jax.ShapeDtypeStruct((m_total, D_MODEL), F32),
                 jax.ShapeDtypeStruct((hist, D_LRU), F32),
                 jax.ShapeDtypeStruct((nb, D_LRU), F32),
                 jax.ShapeDtypeStruct((nb, N_STATE), F32),
                 jax.ShapeDtypeStruct((nb, N_STATE), F32))
    out_specs = (row_block,
                 pl.BlockSpec((hist, D_LRU), lambda i: (0, 0)),
                 pl.BlockSpec((nb, D_LRU), lambda i: (0, 0)),
                 pl.BlockSpec((nb, N_STATE), lambda i: (0, 0)),
                 pl.BlockSpec((nb, N_STATE), lambda i: (0, 0)))
    scratch = [
        pltpu.VMEM((m, D_MODEL), BF16),
        pltpu.VMEM((hist + m, D_LRU), F32),
        pltpu.VMEM((m, D_SSM), F32),
        pltpu.VMEM((m, D_MODEL), F32),
        pltpu.VMEM((m, D_MODEL), F32),
        pltpu.VMEM((m, D_MODEL), F32),
        pltpu.VMEM((m, D_MODEL), BF16),
        pltpu.VMEM((m, N_STATE), F32),
        pltpu.VMEM((m, N_STATE), F32),
        pltpu.VMEM((m, D_LRU), F32),
        pltpu.VMEM((m, D_SSM), F32),
        pltpu.VMEM((m, D_SSM), BF16),
    ]
    return pl.pallas_call(
        functools.partial(_mixer_kernel, nb, tc),
        grid=(m_total // m,),
        in_specs=in_specs,
        out_specs=out_specs,
        out_shape=out_shape,
        scratch_shapes=scratch,
        compiler_params=pltpu.CompilerParams(
            dimension_semantics=("arbitrary",), vmem_limit_bytes=VMEM_LIMIT_BYTES),
        name=f"mixer_nb{nb}_tc{tc}",
    )(x, conv0, h0, sre0, sim0, *params)


def _ffn_kernel(nb, tc,
                x_ref, g0_ref, wup_ref, bup_ref, convw_ref, convb_ref, wdown_ref, bdown_ref,
                ln2g_ref, ln2b_ref,
                o_ref, gnew_ref,
                xb_ref, g_ref, v_ref, hb_ref, t_ref):
    m = nb * tc
    hist = (FFN_CONV - 1) * nb
    step = pl.program_id(0)
    rb = min(ROW_BLOCK, m)
    row_blocks = [pl.ds(off, rb) for off in range(0, m, rb)]

    @pl.when(step == 0)
    def _():
        g_ref[0:hist, :] = g0_ref[...]

    for rows in row_blocks:
        xb_ref[rows, :] = x_ref[rows, :].astype(BF16)

    for s in range(D_FF // FFN_SLAB):
        cols = slice(s * FFN_SLAB, (s + 1) * FFN_SLAB)
        vcols = slice(D_FF + s * FFN_SLAB, D_FF + (s + 1) * FFN_SLAB)
        g_ref[hist:hist + m, cols] = (
            jnp.dot(xb_ref[...], wup_ref[:, cols], preferred_element_type=F32)
            + bup_ref[:, cols])
        v_ref[:, cols] = (
            jnp.dot(xb_ref[...], wup_ref[:, vcols], preferred_element_type=F32)
            + bup_ref[:, vcols])
        for rows in row_blocks:
            acc = convb_ref[:, cols] + convw_ref[0:1, cols] * g_ref[rows, cols]
            for j in range(1, FFN_CONV):
                shifted = pl.ds(rows.start + j * nb, rb)
                acc = acc + convw_ref[j:j + 1, cols] * g_ref[shifted, cols]
            hb_ref[rows, cols] = (_gelu(acc) * v_ref[rows, cols]).astype(BF16)
        part = jnp.dot(hb_ref[:, cols], wdown_ref[cols, :], preferred_element_type=F32)
        if s == 0:
            t_ref[...] = part + bdown_ref[...]
        else:
            t_ref[...] += part

    for rows in row_blocks:
        out = _layer_norm(ALPHA * x_ref[rows, :] + t_ref[rows, :], ln2g_ref[...], ln2b_ref[...])
        if len(o_ref.shape) == 3:
            for k in range(rb // nb):
                o_ref[:, rows.start // nb + k, :] = out[k * nb:(k + 1) * nb, :]
        else:
            o_ref[rows, :] = out

    tail = g_ref[m:m + hist, :]
    g_ref[0:hist, :] = tail
    gnew_ref[...] = tail


def _ffn(x, g0, params, *, nb, tc, batch_major_out=False):
    m_total = x.shape[0]
    m = nb * tc
    assert m_total % m == 0
    hist = (FFN_CONV - 1) * nb
    row_block = pl.BlockSpec((m, D_MODEL), lambda i: (i, 0))
    in_specs = [row_block] + [_whole(None)] * (1 + len(params))
    if batch_major_out:
        y_shape = jax.ShapeDtypeStruct((nb, m_total // nb, D_MODEL), F32)
        y_spec = pl.BlockSpec((nb, tc, D_MODEL), lambda i: (0, i, 0))
    else:
        y_shape = jax.ShapeDtypeStruct((m_total, D_MODEL), F32)
        y_spec = row_block
    out_shape = (y_shape, jax.ShapeDtypeStruct((hist, D_FF), F32))
    out_specs = (y_spec, pl.BlockSpec((hist, D_FF), lambda i: (0, 0)))
    scratch = [
        pltpu.VMEM((m, D_MODEL), BF16),
        pltpu.VMEM((hist + m, D_FF), F32),
        pltpu.VMEM((m, D_FF), F32),
        pltpu.VMEM((m, D_FF), BF16),
        pltpu.VMEM((m, D_MODEL), F32),
    ]
    return pl.pallas_call(
        functools.partial(_ffn_kernel, nb, tc),
        grid=(m_total // m,),
        in_specs=in_specs,
        out_specs=out_specs,
        out_shape=out_shape,
        scratch_shapes=scratch,
        compiler_params=pltpu.CompilerParams(
            dimension_semantics=("arbitrary",), vmem_limit_bytes=VMEM_LIMIT_BYTES),
        name=f"ffn_nb{nb}_tc{tc}",
    )(x, g0, *params)


def _block_diag_mask(rows_per_group, cols_per_group):
    r = jnp.arange(SSM_GROUPS * rows_per_group)[:, None] // rows_per_group
    c = jnp.arange(SSM_GROUPS * cols_per_group)[None, :] // cols_per_group
    return (r == c).astype(F32)


def _ssm_matrices(bb_re_t, bb_im_t, c_re, c_im):
    mask_b = _block_diag_mask(SSM_GROUP, SSM_STATE)
    wb_re = jnp.tile(bb_re_t, (SSM_GROUPS, 1)) * mask_b
    wb_im = jnp.tile(bb_im_t, (SSM_GROUPS, 1)) * mask_b
    kb = D_SSM // SSM_SLABS
    nbk = N_STATE // SSM_SLABS
    wb = jnp.stack([
        jnp.concatenate([wb_re[s * kb:(s + 1) * kb, s * nbk:(s + 1) * nbk],
                         wb_im[s * kb:(s + 1) * kb, s * nbk:(s + 1) * nbk]], axis=1)
        for s in range(SSM_SLABS)]).astype(BF16)

    mask_c = _block_diag_mask(SSM_STATE, SSM_GROUP)

    def c_dense(c):
        ct = c.transpose(0, 2, 1).reshape(N_STATE, SSM_GROUP)
        return jnp.tile(ct, (1, SSM_GROUPS)) * mask_c

    def c_slabs(c):
        cd = c_dense(c)
        return jnp.stack([cd[s * nbk:(s + 1) * nbk, s * kb:(s + 1) * kb]
                          for s in range(SSM_SLABS)]).astype(BF16)

    return wb, c_slabs(c_re), c_slabs(c_im)


def _to_time_major(state):
    nb, k, c = state.shape
    return state.transpose(1, 0, 2).reshape(k * nb, c)


def _from_time_major(state, nb):
    k = state.shape[0] // nb
    return state.reshape(k, nb, state.shape[1]).transpose(1, 0, 2)


def kernel(x_prompt, x_sample, cache_conv_lru, state_lru, state_ssm_re, state_ssm_im, cache_conv_ffn,
           meta_tokens, emb_ln_g, emb_ln_b, w_in, b_in, conv_lru_w, conv_lru_b,
           lru_w_r, lru_b_r, lru_w_i, lru_b_i, lru_lambda,
           ssm_lambda_re, ssm_lambda_im, ssm_log_dt, ssm_b_re, ssm_b_im, ssm_c_re, ssm_c_im, ssm_d,
           w_glu, b_glu, w_out, b_out, ln1_g, ln1_b,
           w_up, b_up, ffn_conv_w, ffn_conv_b, w_down, b_down, ln2_g, ln2_b):
    assert w_in.shape[0] == DEPTH
    nbp, seq, _ = x_prompt.shape
    nbs = x_sample.shape[0]
    row = lambda a: a.reshape(1, -1).astype(F32)

    ab_re, ab_im, bb_re_t, bb_im_t = _ssm_prep(ssm_lambda_re[0], ssm_lambda_im[0], ssm_log_dt[0],
                                                ssm_b_re[0], ssm_b_im[0])
    wb, wc_re, wc_im = _ssm_matrices(bb_re_t, bb_im_t, ssm_c_re[0], ssm_c_im[0])
    wri = jnp.concatenate([lru_w_r[0], lru_w_i[0]], axis=-1).astype(BF16)

    mixer_params = (row(emb_ln_g), row(emb_ln_b), w_in[0].astype(BF16), row(b_in[0]),
                    conv_lru_w[0], row(conv_lru_b[0]),
                    wri, row(lru_b_r[0]), row(lru_b_i[0]), row(lru_lambda[0]),
                    ab_re, ab_im, wb, wc_re, wc_im, row(ssm_d[0]),
                    w_glu[0].astype(BF16), row(b_glu[0]), w_out[0].astype(BF16), row(b_out[0]),
                    row(ln1_g[0]), row(ln1_b[0]))
    ffn_params = (w_up[0].astype(BF16), row(b_up[0]), ffn_conv_w[0], row(ffn_conv_b[0]),
                  w_down[0].astype(BF16), row(b_down[0]), row(ln2_g[0]), row(ln2_b[0]))

    def layer(x, conv0, h0, sre0, sim0, g0, nb, tc):
        x1, conv, h, sre, sim = _mixer(x, conv0, h0, sre0, sim0, mixer_params, nb=nb, tc=tc)
        x2, g = _ffn(x1, g0, ffn_params, nb=nb, tc=tc, batch_major_out=(x.ndim == 3))
        return x2, (conv, h, sre, sim, g)

    zeros = lambda r, c: jnp.zeros((r, c), F32)
    meta = jnp.broadcast_to(meta_tokens[:, None, :].astype(F32), (N_META, nbp, D_MODEL))
    _, st = layer(meta.reshape(N_META * nbp, D_MODEL),
                  zeros((LRU_CONV - 1) * nbp, D_LRU), zeros(nbp, D_LRU),
                  zeros(nbp, N_STATE), zeros(nbp, N_STATE), zeros((FFN_CONV - 1) * nbp, D_FF),
                  nbp, N_META)
    y_prompt, st_p = layer(x_prompt, *st, nbp, PROMPT_STEPS_PER_CHUNK)

    ys, st_s = layer(x_sample.reshape(nbs, D_MODEL),
                     _to_time_major(cache_conv_lru[0]), state_lru[0],
                     state_ssm_re[0].reshape(nbs, N_STATE), state_ssm_im[0].reshape(nbs, N_STATE),
                     _to_time_major(cache_conv_ffn[0]), nbs, 1)
    y_sample = ys.reshape(nbs, 1, D_MODEL)

    def states(st, nb):
        conv, h, sre, sim, g = st
        return (_from_time_major(conv, nb)[None], h[None],
                sre.reshape(1, nb, SSM_GROUPS, SSM_STATE), sim.reshape(1, nb, SSM_GROUPS, SSM_STATE),
                _from_time_major(g, nb)[None])

    return (y_prompt, y_sample) + states(st_p, nbp) + states(st_s, nbs)
```

```python
import functools
import math

import jax
import jax.numpy as jnp
from jax import lax
from jax.experimental import pallas as pl
from jax.experimental.pallas import tpu as pltpu

D_MODEL = 1024
N_META = 16
D_LRU = D_MODEL
LRU_BLOCKS = 8
LRU_BLOCK = D_LRU // LRU_BLOCKS
LRU_CONV = 4
LRU_C = 8.0
D_SSM = D_MODEL // 2
SSM_GROUP = 16
SSM_GROUPS = D_SSM // SSM_GROUP
SSM_STATE = 64
N_STATE = SSM_GROUPS * SSM_STATE
D_FF = 3 * D_MODEL
FFN_CONV = 3
D_IN = D_LRU + D_SSM + 2 * D_MODEL
DEPTH = 1
ALPHA = (2.0 * DEPTH) ** 0.25
LN_EPS = 1e-5
GELU_C = math.sqrt(2.0 / math.pi)

V7X_VMEM_BYTES = 64 * 1024 * 1024
VMEM_LIMIT_BYTES = V7X_VMEM_BYTES - 8 * 1024 * 1024
SUBLANES = 8
ROW_BLOCK = 256
GATE_ROWS = 64
SSM_SLABS = 2
SSM_LANE_BLOCK = 512
PROMPT_STEPS_PER_CHUNK = 64
FFN_SLAB = 1024

BF16 = jnp.bfloat16
F32 = jnp.float32


def _layer_norm(x, g, b):
    mu = jnp.mean(x, axis=-1, keepdims=True)
    xc = x - mu
    var = jnp.mean(xc * xc, axis=-1, keepdims=True)
    return xc * lax.rsqrt(var + LN_EPS) * g + b


def _gelu(x):
    return 0.5 * x * (1.0 + jnp.tanh(GELU_C * (x + 0.044715 * (x * x * x))))


def _sigmoid(x):
    return 0.5 * jnp.tanh(0.5 * x) + 0.5


def _softplus(x):
    return jnp.maximum(x, 0.0) + jnp.log1p(jnp.exp(-jnp.abs(x)))


def _for_row_blocks(n_rows, fn):
    rb = min(ROW_BLOCK, n_rows)
    assert n_rows % rb == 0
    for off in range(0, n_rows, rb):
        fn(off, rb)


def _ssm_prep_kernel(lre_ref, lim_ref, logdt_ref, bre_ref, bim_ref,
                     abre_ref, abim_ref, bbre_ref, bbim_ref):
    lre = lre_ref[...]
    lim = lim_ref[...]
    dt = jnp.exp(logdt_ref[...])
    mag = jnp.exp(lre * dt)
    ab_re = mag * jnp.cos(lim * dt)
    ab_im = mag * jnp.sin(lim * dt)
    den = lre * lre + lim * lim
    nr = ab_re - 1.0
    ni = ab_im
    f_re = (nr * lre + ni * lim) / den
    f_im = (ni * lre - nr * lim) / den
    br = bre_ref[...]
    bi = bim_ref[...]
    abre_ref[...] = ab_re
    abim_ref[...] = ab_im
    bbre_ref[...] = f_re * br - f_im * bi
    bbim_ref[...] = f_re * bi + f_im * br


def _ssm_prep(lam_re, lam_im, log_dt, b_re, b_im):
    row = lambda a: a.reshape(1, N_STATE)
    b_t = lambda a: a.transpose(2, 0, 1).reshape(SSM_GROUP, N_STATE)
    logdt = jnp.repeat(log_dt, SSM_STATE).reshape(1, N_STATE)
    out_shape = (jax.ShapeDtypeStruct((1, N_STATE), F32),) * 2 + (
        jax.ShapeDtypeStruct((SSM_GROUP, N_STATE), F32),) * 2
    return pl.pallas_call(_ssm_prep_kernel, out_shape=out_shape, name="ssm_prep")(
        row(lam_re), row(lam_im), logdt, b_t(b_re), b_t(b_im))


def _mixer_kernel(nb, tc,
                  x_ref, conv0_ref, h0_ref, sre0_ref, sim0_ref,
                  embg_ref, embb_ref, win_ref, bin_ref, convw_ref, convb_ref,
                  wri_ref, br_ref, bi_ref, lam_ref,
                  abre_ref, abim_ref, wb_ref, wcre_ref, wcim_ref, ssmd_ref,
                  wglu_ref, bglu_ref, wout_ref, bout_ref, ln1g_ref, ln1b_ref,
                  o_ref, conv_ref, h_ref, sre_ref, sim_ref,
                  zb_ref, xa_ref, us_ref, ga_ref, gb_ref, t1_ref, cb_ref, p_ref, q_ref,
                  ya_ref, y_ref, glb_ref):
    m = nb * tc
    hist = (LRU_CONV - 1) * nb
    step = pl.program_id(0)

    @pl.when(step == 0)
    def _():
        xa_ref[0:hist, :] = conv0_ref[...]
        h_ref[...] = h0_ref[...]
        sre_ref[...] = sre0_ref[...]
        sim_ref[...] = sim0_ref[...]

    if len(x_ref.shape) == 3:
        for t in range(tc):
            o_ref[pl.ds(t * nb, nb), :] = x_ref[:, t, :]
        src_ref = o_ref
    else:
        src_ref = x_ref

    def ln_in(off, rb):
        z = _layer_norm(src_ref[pl.ds(off, rb), :], embg_ref[...], embb_ref[...])
        o_ref[pl.ds(off, rb), :] = z
        zb_ref[pl.ds(off, rb), :] = z.astype(BF16)

    _for_row_blocks(m, ln_in)

    def proj(lo, hi):
        return (jnp.dot(zb_ref[...], win_ref[:, lo:hi], preferred_element_type=F32)
                + bin_ref[:, lo:hi])

    c0, c1, c2 = D_LRU, D_LRU + D_SSM, D_LRU + D_SSM + D_MODEL
    us_ref[...] = proj(c0, c1)

    kb = D_SSM // SSM_SLABS
    nbk = N_STATE // SSM_SLABS
    for s in range(SSM_SLABS):
        bu = jnp.dot(us_ref[:, s * kb:(s + 1) * kb].astype(BF16), wb_ref[s],
                     preferred_element_type=F32)
        p_ref[:, s * nbk:(s + 1) * nbk] = bu[:, :nbk]
        q_ref[:, s * nbk:(s + 1) * nbk] = bu[:, nbk:]

    xa_ref[hist:hist + m, :] = proj(0, c0)

    if tc == 1:
        def ssm_step(off, rb):
            rows = pl.ds(off, rb)
            ar, ai = abre_ref[...], abim_ref[...]
            hr, hi = sre0_ref[rows, :], sim0_ref[rows, :]
            nr = ar * hr - ai * hi + p_ref[rows, :]
            ni = ar * hi + ai * hr + q_ref[rows, :]
            p_ref[rows, :] = nr
            q_ref[rows, :] = ni
            sre_ref[rows, :] = nr
            sim_ref[rows, :] = ni

        for off in range(0, m, SUBLANES):
            ssm_step(off, SUBLANES)
    else:
        assert nb == SUBLANES
        for lb in range(N_STATE // SSM_LANE_BLOCK):
            lanes = slice(lb * SSM_LANE_BLOCK, (lb + 1) * SSM_LANE_BLOCK)
            ar = jnp.broadcast_to(abre_ref[:, lanes], (nb, SSM_LANE_BLOCK))
            ai = jnp.broadcast_to(abim_ref[:, lanes], (nb, SSM_LANE_BLOCK))
            hr, hi = sre_ref[:, lanes], sim_ref[:, lanes]
            for t in range(tc):
                rows = pl.ds(t * nb, nb)
                hr, hi = (ar * hr - ai * hi + p_ref[rows, lanes],
                          ar * hi + ai * hr + q_ref[rows, lanes])
                p_ref[rows, lanes] = hr
                q_ref[rows, lanes] = hi
            sre_ref[:, lanes] = hr
            sim_ref[:, lanes] = hi

    def conv(off, rb):
        acc = convb_ref[...] + convw_ref[0:1, :] * xa_ref[pl.ds(off, rb), :]
        for j in range(1, LRU_CONV):
            acc = acc + convw_ref[j:j + 1, :] * xa_ref[pl.ds(off + j * nb, rb), :]
        t1_ref[pl.ds(off, rb), :] = acc
        cb_ref[pl.ds(off, rb), :] = acc.astype(BF16)

    _for_row_blocks(m, conv)

    kc = D_SSM // SSM_SLABS
    for s in range(SSM_SLABS):
        hs = slice(s * nbk, (s + 1) * nbk)
        y_ref[:, s * kc:(s + 1) * kc] = (
            jnp.dot(p_ref[:, hs].astype(BF16), wcre_ref[s], preferred_element_type=F32)
            - jnp.dot(q_ref[:, hs].astype(BF16), wcim_ref[s], preferred_element_type=F32))

    def ssm_out(off, rb):
        rows = pl.ds(off, rb)
        ys = y_ref[rows, :] + ssmd_ref[...] * us_ref[rows, :]
        glb_ref[rows, :] = _gelu(ys).astype(BF16)

    _for_row_blocks(m, ssm_out)

    for n in range(LRU_BLOCKS):
        q_ref[:, 2 * n * LRU_BLOCK:(2 * n + 2) * LRU_BLOCK] = jnp.dot(
            cb_ref[:, n * LRU_BLOCK:(n + 1) * LRU_BLOCK], wri_ref[n],
            preferred_element_type=F32)
    ga_ref[...] = proj(c1, c2)
    p_ref[...] = (jnp.dot(glb_ref[...], wglu_ref[...], preferred_element_type=F32)
                  + bglu_ref[...])
    gb_ref[...] = proj(c2, D_IN)

    sp = _softplus(-lam_ref[...])
    rb = min(GATE_ROWS, m)
    hs = [None if tc == 1 else h_ref[:, n * LRU_BLOCK:(n + 1) * LRU_BLOCK]
          for n in range(LRU_BLOCKS)]
    for off in range(0, m, rb):
        rows = pl.ds(off, rb)
        for n in range(LRU_BLOCKS):
            cols = slice(n * LRU_BLOCK, (n + 1) * LRU_BLOCK)
            rcols = slice(2 * n * LRU_BLOCK, (2 * n + 1) * LRU_BLOCK)
            icols = slice((2 * n + 1) * LRU_BLOCK, (2 * n + 2) * LRU_BLOCK)
            r = _sigmoid(q_ref[rows, rcols] + br_ref[:, cols])
            i = _sigmoid(q_ref[rows, icols] + bi_ref[:, cols])
            log_a = (-LRU_C * r) * sp[:, cols]
            a = jnp.exp(log_a)
            mult = jnp.sqrt(-jnp.tanh(log_a) * (a * a + 1.0))
            bx = mult * (i * t1_ref[rows, cols])
            if tc == 1:
                hn = a * h0_ref[rows, cols] + bx
                ya_ref[rows, cols] = hn
                h_ref[rows, cols] = hn
            else:
                h = hs[n]
                for k in range(0, rb, nb):
                    h = a[k:k + nb, :] * h + bx[k:k + nb, :]
                    ya_ref[pl.ds(off + k, nb), cols] = h
                hs[n] = h
    if tc != 1:
        for n in range(LRU_BLOCKS):
            h_ref[:, n * LRU_BLOCK:(n + 1) * LRU_BLOCK] = hs[n]

    def merge(off, rb):
        rows = pl.ds(off, rb)
        yb = p_ref[rows, 0:D_MODEL] * _sigmoid(p_ref[rows, D_MODEL:2 * D_MODEL])
        mix = _sigmoid(ga_ref[rows, :]) * ya_ref[rows, :] + _sigmoid(gb_ref[rows, :]) * yb
        cb_ref[rows, :] = mix.astype(BF16)

    _for_row_blocks(m, merge)

    half = m // 2
    for h0 in (0, half):
        hrows = pl.ds(h0, half)
        t1_ref[hrows, :] = (jnp.dot(cb_ref[hrows, :], wout_ref[...], preferred_element_type=F32)
                            + bout_ref[...])

        def ln_out(off, rb, h0=h0):
            rows = pl.ds(h0 + off, rb)
            o_ref[rows, :] = _layer_norm(ALPHA * o_ref[rows, :] + t1_ref[rows, :],
                                         ln1g_ref[...], ln1b_ref[...])

        _for_row_blocks(half, ln_out)

    tail = xa_ref[m:m + hist, :]
    xa_ref[0:hist, :] = tail
    conv_ref[...] = tail


def _whole(shape):
    return pl.BlockSpec(memory_space=pltpu.VMEM)


def _mixer(x, conv0, h0, sre0, sim0, params, *, nb, tc):
    m = nb * tc
    row_block = pl.BlockSpec((m, D_MODEL), lambda i: (i, 0))
    if x.ndim == 3:
        assert x.shape[0] == nb and x.shape[1] % tc == 0
        m_total = nb * x.shape[1]
        x_spec = pl.BlockSpec((nb, tc, D_MODEL), lambda i: (0, i, 0))
    else:
        m_total = x.shape[0]
        x_spec = row_block
    assert m_total % m == 0
    hist = (LRU_CONV - 1) * nb
    in_specs = [x_spec] + [_whole(None)] * (4 + len(params))
    out_shape = (jax.ShapeDtypeStruct((m_total, D_MODEL), F32),
                 jax.ShapeDtypeStruct((hist, D_LRU), F32),
                 jax.ShapeDtypeStruct((nb, D_LRU), F32),
                 jax.ShapeDtypeStruct((nb, N_STATE), F32),
                 jax.ShapeDtypeStruct((nb, N_STATE), F32))
    out_specs = (row_block,
                 pl.BlockSpec((hist, D_LRU), lambda i: (0, 0)),
                 pl.BlockSpec((nb, D_LRU), lambda i: (0, 0)),
                 pl.BlockSpec((nb, N_STATE), lambda i: (0, 0)),
                 pl.BlockSpec((nb, N_STATE), lambda i: (0, 0)))
    scratch = [
        pltpu.VMEM((m, D_MODEL), BF16),
        pltpu.VMEM((hist + m, D_LRU), F32),
        pltpu.VMEM((m, D_SSM), F32),
        pltpu.VMEM((m, D_MODEL), F32),
        pltpu.VMEM((m, D_MODEL), F32),
        pltpu.VMEM((m, D_MODEL), F32),
        pltpu.VMEM((m, D_MODEL), BF16),
        pltpu.VMEM((m, N_STATE), F32),
        pltpu.VMEM((m, N_STATE), F32),
        pltpu.VMEM((m, D_LRU), F32),
        pltpu.VMEM((m, D_SSM), F32),
        pltpu.VMEM((m, D_SSM), BF16),
    ]
    return pl.pallas_call(
        functools.partial(_mixer_kernel, nb, tc),
        grid=(m_total // m,),
        in_specs=in_specs,
        out_specs=out_specs,
        out_shape=out_shape,
        scratch_shapes=scratch,
        compiler_params=pltpu.CompilerParams(
            dimension_semantics=("arbitrary",), vmem_limit_bytes=VMEM_LIMIT_BYTES),
        name=f"mixer_nb{nb}_tc{tc}",
    )(x, conv0, h0, sre0, sim0, *params)


def _ffn_kernel(nb, tc,
                x_ref, g0_ref, wup_ref, bup_ref, convw_ref, convb_ref, wdown_ref, bdown_ref,
                ln2g_ref, ln2b_ref,
                o_ref, gnew_ref,
                xb_ref, g_ref, v_ref, hb_ref, t_ref):
    m = nb * tc
    hist = (FFN_CONV - 1) * nb
    step = pl.program_id(0)
    rb = min(ROW_BLOCK, m)
    row_blocks = [pl.ds(off, rb) for off in range(0, m, rb)]

    @pl.when(step == 0)
    def _():
        g_ref[0:hist, :] = g0_ref[...]

    for rows in row_blocks:
        xb_ref[rows, :] = x_ref[rows, :].astype(BF16)

    for s in range(D_FF // FFN_SLAB):
        cols = slice(s * FFN_SLAB, (s + 1) * FFN_SLAB)
        vcols = slice(D_FF + s * FFN_SLAB, D_FF + (s + 1) * FFN_SLAB)
        g_ref[hist:hist + m, cols] = (
            jnp.dot(xb_ref[...], wup_ref[:, cols], preferred_element_type=F32)
            + bup_ref[:, cols])
        v_ref[:, cols] = (
            jnp.dot(xb_ref[...], wup_ref[:, vcols], preferred_element_type=F32)
            + bup_ref[:, vcols])
        for rows in row_blocks:
            acc = convb_ref[:, cols] + convw_ref[0:1, cols] * g_ref[rows, cols]
            for j in range(1, FFN_CONV):
                shifted = pl.ds(rows.start + j * nb, rb)
                acc = acc + convw_ref[j:j + 1, cols] * g_ref[shifted, cols]
            hb_ref[rows, cols] = (_gelu(acc) * v_ref[rows, cols]).astype(BF16)
        part = jnp.dot(hb_ref[:, cols], wdown_ref[cols, :], preferred_element_type=F32)
        if s == 0:
            t_ref[...] = part + bdown_ref[...]
        else:
            t_ref[...] += part

    for rows in row_blocks:
        out = _layer_norm(ALPHA * x_ref[rows, :] + t_ref[rows, :], ln2g_ref[...], ln2b_ref[...])
        if len(o_ref.shape) == 3:
            for k in range(rb // nb):
                o_ref[:, rows.start // nb + k, :] = out[k * nb:(k + 1) * nb, :]
        else:
            o_ref[rows, :] = out

    tail = g_ref[m:m + hist, :]
    g_ref[0:hist, :] = tail
    gnew_ref[...] = tail


def _ffn(x, g0, params, *, nb, tc, batch_major_out=False):
    m_total = x.shape[0]
    m = nb * tc
    assert m_total % m == 0
    hist = (FFN_CONV - 1) * nb
    row_block = pl.BlockSpec((m, D_MODEL), lambda i: (i, 0))
    in_specs = [row_block] + [_whole(None)] * (1 + len(params))
    if batch_major_out:
        y_shape = jax.ShapeDtypeStruct((nb, m_total // nb, D_MODEL), F32)
        y_spec = pl.BlockSpec((nb, tc, D_MODEL), lambda i: (0, i, 0))
    else:
        y_shape = jax.ShapeDtypeStruct((m_total, D_MODEL), F32)
        y_spec = row_block
    out_shape = (y_shape, jax.ShapeDtypeStruct((hist, D_FF), F32))
    out_specs = (y_spec, pl.BlockSpec((hist, D_FF), lambda i: (0, 0)))
    scratch = [
        pltpu.VMEM((m, D_MODEL), BF16),
        pltpu.VMEM((hist + m, D_FF), F32),
        pltpu.VMEM((m, D_FF), F32),
        pltpu.VMEM((m, D_FF), BF16),
        pltpu.VMEM((m, D_MODEL), F32),
    ]
    return pl.pallas_call(
        functools.partial(_ffn_kernel, nb, tc),
        grid=(m_total // m,),
        in_specs=in_specs,
        out_specs=out_specs,
        out_shape=out_shape,
        scratch_shapes=scratch,
        compiler_params=pltpu.CompilerParams(
            dimension_semantics=("arbitrary",), vmem_limit_bytes=VMEM_LIMIT_BYTES),
        name=f"ffn_nb{nb}_tc{tc}",
    )(x, g0, *params)


def _block_diag_mask(rows_per_group, cols_per_group):
    r = jnp.arange(SSM_GROUPS * rows_per_group)[:, None] // rows_per_group
    c = jnp.arange(SSM_GROUPS * cols_per_group)[None, :] // cols_per_group
    return (r == c).astype(F32)


def _ssm_matrices(bb_re_t, bb_im_t, c_re, c_im):
    mask_b = _block_diag_mask(SSM_GROUP, SSM_STATE)
    wb_re = jnp.tile(bb_re_t, (SSM_GROUPS, 1)) * mask_b
    wb_im = jnp.tile(bb_im_t, (SSM_GROUPS, 1)) * mask_b
    kb = D_SSM // SSM_SLABS
    nbk = N_STATE // SSM_SLABS
    wb = jnp.stack([
        jnp.concatenate([wb_re[s * kb:(s + 1) * kb, s * nbk:(s + 1) * nbk],
                         wb_im[s * kb:(s + 1) * kb, s * nbk:(s + 1) * nbk]], axis=1)
        for s in range(SSM_SLABS)]).astype(BF16)

    mask_c = _block_diag_mask(SSM_STATE, SSM_GROUP)

    def c_dense(c):
        ct = c.transpose(0, 2, 1).reshape(N_STATE, SSM_GROUP)
        return jnp.tile(ct, (1, SSM_GROUPS)) * mask_c

    def c_slabs(c):
        cd = c_dense(c)
        return jnp.stack([cd[s * nbk:(s + 1) * nbk, s * kb:(s + 1) * kb]
                          for s in range(SSM_SLABS)]).astype(BF16)

    return wb, c_slabs(c_re), c_slabs(c_im)


def _to_time_major(state):
    nb, k, c = state.shape
    return state.transpose(1, 0, 2).reshape(k * nb, c)


def _from_time_major(state, nb):
    k = state.shape[0] // nb
    return state.reshape(k, nb, state.shape[1]).transpose(1, 0, 2)


def kernel(x_prompt, x_sample, cache_conv_lru, state_lru, state_ssm_re, state_ssm_im, cache_conv_ffn,
           meta_tokens, emb_ln_g, emb_ln_b, w_in, b_in, conv_lru_w, conv_lru_b,
           lru_w_r, lru_b_r, lru_w_i, lru_b_i, lru_lambda,
           ssm_lambda_re, ssm_lambda_im, ssm_log_dt, ssm_b_re, ssm_b_im, ssm_c_re, ssm_c_im, ssm_d,
           w_glu, b_glu, w_out, b_out, ln1_g, ln1_b,
           w_up, b_up, ffn_conv_w, ffn_conv_b, w_down, b_down, ln2_g, ln2_b):
    assert w_in.shape[0] == DEPTH
    nbp, seq, _ = x_prompt.shape
    nbs = x_sample.shape[0]
    row = lambda a: a.reshape(1, -1).astype(F32)

    ab_re, ab_im, bb_re_t, bb_im_t = _ssm_prep(ssm_lambda_re[0], ssm_lambda_im[0], ssm_log_dt[0],
                                                ssm_b_re[0], ssm_b_im[0])
    wb, wc_re, wc_im = _ssm_matrices(bb_re_t, bb_im_t, ssm_c_re[0], ssm_c_im[0])
    wri = jnp.concatenate([lru_w_r[0], lru_w_i[0]], axis=-1).astype(BF16)

    mixer_params = (row(emb_ln_g), row(emb_ln_b), w_in[0].astype(BF16), row(b_in[0]),
                    conv_lru_w[0], row(conv_lru_b[0]),
                    wri, row(lru_b_r[0]), row(lru_b_i[0]), row(lru_lambda[0]),
                    ab_re, ab_im, wb, wc_re, wc_im, row(ssm_d[0]),
                    w_glu[0].astype(BF16), row(b_glu[0]), w_out[0].astype(BF16), row(b_out[0]),
                    row(ln1_g[0]), row(ln1_b[0]))
    ffn_params = (w_up[0].astype(BF16), row(b_up[0]), ffn_conv_w[0], row(ffn_conv_b[0]),
                  w_down[0].astype(BF16), row(b_down[0]), row(ln2_g[0]), row(ln2_b[0]))

    def layer(x, conv0, h0, sre0, sim0, g0, nb, tc):
        x1, conv, h, sre, sim = _mixer(x, conv0, h0, sre0, sim0, mixer_params, nb=nb, tc=tc)
        x2, g = _ffn(x1, g0, ffn_params, nb=nb, tc=tc, batch_major_out=(x.ndim == 3))
        return x2, (conv, h, sre, sim, g)

    zeros = lambda r, c: jnp.zeros((r, c), F32)
    meta = jnp.broadcast_to(meta_tokens[:, None, :].astype(F32), (N_META, nbp, D_MODEL))
    _, st = layer(meta.reshape(N_META * nbp, D_MODEL),
                  zeros((LRU_CONV - 1) * nbp, D_LRU), zeros(nbp, D_LRU),
                  zeros(nbp, N_STATE), zeros(nbp, N_STATE), zeros((FFN_CONV - 1) * nbp, D_FF),
                  nbp, N_META)
    y_prompt, st_p = layer(x_prompt, *st, nbp, PROMPT_STEPS_PER_CHUNK)

    ys, st_s = layer(x_sample.reshape(nbs, D_MODEL),
                     _to_time_major(cache_conv_lru[0]), state_lru[0],
                     state_ssm_re[0].reshape(nbs, N_STATE), state_ssm_im[0].reshape(nbs, N_STATE),
                     _to_time_major(cache_conv_ffn[0]), nbs, 1)
    y_sample = ys.reshape(nbs, 1, D_MODEL)

    def states(st, nb):
        conv, h, sre, sim, g = st
        return (_from_time_major(conv, nb)[None], h[None],
                sre.reshape(1, nb, SSM_GROUPS, SSM_STATE), sim.reshape(1, nb, SSM_GROUPS, SSM_STATE),
                _from_time_major(g, nb)[None])

    return (y_prompt, y_sample) + states(st_p, nbp) + states(st_s, nbs)
```

```python
import functools
import math

import jax
import jax.numpy as jnp
from jax import lax
from jax.experimental import pallas as pl
from jax.experimental.pallas import tpu as pltpu

D_MODEL = 1024
N_META = 16
D_LRU = D_MODEL
LRU_BLOCKS = 8
LRU_BLOCK = D_LRU // LRU_BLOCKS
LRU_CONV = 4
LRU_C = 8.0
D_SSM = D_MODEL // 2
SSM_GROUP = 16
SSM_GROUPS = D_SSM // SSM_GROUP
SSM_STATE = 64
N_STATE = SSM_GROUPS * SSM_STATE
D_FF = 3 * D_MODEL
FFN_CONV = 3
D_IN = D_LRU + D_SSM + 2 * D_MODEL
DEPTH = 1
ALPHA = (2.0 * DEPTH) ** 0.25
LN_EPS = 1e-5
GELU_C = math.sqrt(2.0 / math.pi)

V7X_VMEM_BYTES = 64 * 1024 * 1024
VMEM_LIMIT_BYTES = V7X_VMEM_BYTES - 8 * 1024 * 1024
SUBLANES = 8
ROW_BLOCK = 512
SSM_SLABS = 2
SSM_LANE_BLOCK = 512
PROMPT_STEPS_PER_CHUNK = 64
FFN_SLAB = 1024

BF16 = jnp.bfloat16
F32 = jnp.float32


def _layer_norm(x, g, b):
    mu = jnp.mean(x, axis=-1, keepdims=True)
    xc = x - mu
    var = jnp.mean(xc * xc, axis=-1, keepdims=True)
    return xc * lax.rsqrt(var + LN_EPS) * g + b


def _gelu(x):
    return 0.5 * x * (1.0 + jnp.tanh(GELU_C * (x + 0.044715 * (x * x * x))))


def _sigmoid(x):
    return 0.5 * jnp.tanh(0.5 * x) + 0.5


def _softplus(x):
    return jnp.maximum(x, 0.0) + jnp.log1p(jnp.exp(-jnp.abs(x)))


def _for_row_blocks(n_rows, fn):
    rb = min(ROW_BLOCK, n_rows)
    assert n_rows % rb == 0
    for off in range(0, n_rows, rb):
        fn(off, rb)


def _ssm_prep_kernel(lre_ref, lim_ref, logdt_ref, bre_ref, bim_ref,
                     abre_ref, abim_ref, bbre_ref, bbim_ref):
    lre = lre_ref[...]
    lim = lim_ref[...]
    dt = jnp.exp(logdt_ref[...])
    mag = jnp.exp(lre * dt)
    ab_re = mag * jnp.cos(lim * dt)
    ab_im = mag * jnp.sin(lim * dt)
    den = lre * lre + lim * lim
    nr = ab_re - 1.0
    ni = ab_im
    f_re = (nr * lre + ni * lim) / den
    f_im = (ni * lre - nr * lim) / den
    br = bre_ref[...]
    bi = bim_ref[...]
    abre_ref[...] = ab_re
    abim_ref[...] = ab_im
    bbre_ref[...] = f_re * br - f_im * bi
    bbim_ref[...] = f_re * bi + f_im * br


def _ssm_prep(lam_re, lam_im, log_dt, b_re, b_im):
    row = lambda a: a.reshape(1, N_STATE)
    b_t = lambda a: a.transpose(2, 0, 1).reshape(SSM_GROUP, N_STATE)
    logdt = jnp.repeat(log_dt, SSM_STATE).reshape(1, N_STATE)
    out_shape = (jax.ShapeDtypeStruct((1, N_STATE), F32),) * 2 + (
        jax.ShapeDtypeStruct((SSM_GROUP, N_STATE), F32),) * 2
    return pl.pallas_call(_ssm_prep_kernel, out_shape=out_shape, name="ssm_prep")(
        row(lam_re), row(lam_im), logdt, b_t(b_re), b_t(b_im))


def _mixer_kernel(nb, tc,
                  x_ref, conv0_ref, h0_ref, sre0_ref, sim0_ref,
                  embg_ref, embb_ref, win_ref, bin_ref, convw_ref, convb_ref,
                  wri_ref, br_ref, bi_ref, lam_ref,
                  abre_ref, abim_ref, wb_ref, wcre_ref, wcim_ref, ssmd_ref,
                  wglu_ref, bglu_ref, wout_ref, bout_ref, ln1g_ref, ln1b_ref,
                  o_ref, conv_ref, h_ref, sre_ref, sim_ref,
                  zb_ref, xa_ref, us_ref, ga_ref, gb_ref, t1_ref, cb_ref, p_ref, q_ref,
                  ya_ref, y_ref, glb_ref):
    m = nb * tc
    hist = (LRU_CONV - 1) * nb
    step = pl.program_id(0)

    @pl.when(step == 0)
    def _():
        xa_ref[0:hist, :] = conv0_ref[...]
        h_ref[...] = h0_ref[...]
        sre_ref[...] = sre0_ref[...]
        sim_ref[...] = sim0_ref[...]

    if len(x_ref.shape) == 3:
        for t in range(tc):
            o_ref[pl.ds(t * nb, nb), :] = x_ref[:, t, :]
        src_ref = o_ref
    else:
        src_ref = x_ref

    def ln_in(off, rb):
        z = _layer_norm(src_ref[pl.ds(off, rb), :], embg_ref[...], embb_ref[...])
        o_ref[pl.ds(off, rb), :] = z
        zb_ref[pl.ds(off, rb), :] = z.astype(BF16)

    _for_row_blocks(m, ln_in)

    def proj(lo, hi):
        return (jnp.dot(zb_ref[...], win_ref[:, lo:hi], preferred_element_type=F32)
                + bin_ref[:, lo:hi])

    c0, c1, c2 = D_LRU, D_LRU + D_SSM, D_LRU + D_SSM + D_MODEL
    us_ref[...] = proj(c0, c1)

    kb = D_SSM // SSM_SLABS
    nbk = N_STATE // SSM_SLABS
    for s in range(SSM_SLABS):
        bu = jnp.dot(us_ref[:, s * kb:(s + 1) * kb].astype(BF16), wb_ref[s],
                     preferred_element_type=F32)
        p_ref[:, s * nbk:(s + 1) * nbk] = bu[:, :nbk]
        q_ref[:, s * nbk:(s + 1) * nbk] = bu[:, nbk:]

    xa_ref[hist:hist + m, :] = proj(0, c0)

    if tc == 1:
        def ssm_step(off, rb):
            rows = pl.ds(off, rb)
            ar, ai = abre_ref[...], abim_ref[...]
            hr, hi = sre0_ref[rows, :], sim0_ref[rows, :]
            nr = ar * hr - ai * hi + p_ref[rows, :]
            ni = ar * hi + ai * hr + q_ref[rows, :]
            p_ref[rows, :] = nr
            q_ref[rows, :] = ni
            sre_ref[rows, :] = nr
            sim_ref[rows, :] = ni

        for off in range(0, m, SUBLANES):
            ssm_step(off, SUBLANES)
    else:
        assert nb == SUBLANES
        for lb in range(N_STATE // SSM_LANE_BLOCK):
            lanes = slice(lb * SSM_LANE_BLOCK, (lb + 1) * SSM_LANE_BLOCK)
            ar = jnp.broadcast_to(abre_ref[:, lanes], (nb, SSM_LANE_BLOCK))
            ai = jnp.broadcast_to(abim_ref[:, lanes], (nb, SSM_LANE_BLOCK))
            hr, hi = sre_ref[:, lanes], sim_ref[:, lanes]
            for t in range(tc):
                rows = pl.ds(t * nb, nb)
                hr, hi = (ar * hr - ai * hi + p_ref[rows, lanes],
                          ar * hi + ai * hr + q_ref[rows, lanes])
                p_ref[rows, lanes] = hr
                q_ref[rows, lanes] = hi
            sre_ref[:, lanes] = hr
            sim_ref[:, lanes] = hi

    def conv(off, rb):
        acc = convb_ref[...] + convw_ref[0:1, :] * xa_ref[pl.ds(off, rb), :]
        for j in range(1, LRU_CONV):
            acc = acc + convw_ref[j:j + 1, :] * xa_ref[pl.ds(off + j * nb, rb), :]
        t1_ref[pl.ds(off, rb), :] = acc
        cb_ref[pl.ds(off, rb), :] = acc.astype(BF16)

    _for_row_blocks(m, conv)

    kc = D_SSM // SSM_SLABS
    for s in range(SSM_SLABS):
        hs = slice(s * nbk, (s + 1) * nbk)
        y_ref[:, s * kc:(s + 1) * kc] = (
            jnp.dot(p_ref[:, hs].astype(BF16), wcre_ref[s], preferred_element_type=F32)
            - jnp.dot(q_ref[:, hs].astype(BF16), wcim_ref[s], preferred_element_type=F32))

    def ssm_out(off, rb):
        rows = pl.ds(off, rb)
        ys = y_ref[rows, :] + ssmd_ref[...] * us_ref[rows, :]
        glb_ref[rows, :] = _gelu(ys).astype(BF16)

    _for_row_blocks(m, ssm_out)

    for n in range(LRU_BLOCKS):
        q_ref[:, 2 * n * LRU_BLOCK:(2 * n + 2) * LRU_BLOCK] = jnp.dot(
            cb_ref[:, n * LRU_BLOCK:(n + 1) * LRU_BLOCK], wri_ref[n],
            preferred_element_type=F32)
    ga_ref[...] = proj(c1, c2)
    p_ref[...] = (jnp.dot(glb_ref[...], wglu_ref[...], preferred_element_type=F32)
                  + bglu_ref[...])
    gb_ref[...] = proj(c2, D_IN)

    sp = _softplus(-lam_ref[...])
    rb = min(ROW_BLOCK, m)
    for n in range(LRU_BLOCKS):
        cols = slice(n * LRU_BLOCK, (n + 1) * LRU_BLOCK)
        rcols = slice(2 * n * LRU_BLOCK, (2 * n + 1) * LRU_BLOCK)
        icols = slice((2 * n + 1) * LRU_BLOCK, (2 * n + 2) * LRU_BLOCK)
        h = None if tc == 1 else h_ref[:, cols]
        for off in range(0, m, rb):
            rows = pl.ds(off, rb)
            r = _sigmoid(q_ref[rows, rcols] + br_ref[:, cols])
            i = _sigmoid(q_ref[rows, icols] + bi_ref[:, cols])
            log_a = (-LRU_C * r) * sp[:, cols]
            a = jnp.exp(log_a)
            mult = jnp.sqrt(-jnp.tanh(log_a) * (a * a + 1.0))
            bx = mult * (i * t1_ref[rows, cols])
            if tc == 1:
                hn = a * h0_ref[rows, cols] + bx
                ya_ref[rows, cols] = hn
                h_ref[rows, cols] = hn
            else:
                for k in range(0, rb, nb):
                    h = a[k:k + nb, :] * h + bx[k:k + nb, :]
                    ya_ref[pl.ds(off + k, nb), cols] = h
        if tc != 1:
            h_ref[:, cols] = h

    def merge(off, rb):
        rows = pl.ds(off, rb)
        yb = p_ref[rows, 0:D_MODEL] * _sigmoid(p_ref[rows, D_MODEL:2 * D_MODEL])
        mix = _sigmoid(ga_ref[rows, :]) * ya_ref[rows, :] + _sigmoid(gb_ref[rows, :]) * yb
        cb_ref[rows, :] = mix.astype(BF16)

    _for_row_blocks(m, merge)

    half = m // 2
    for h0 in (0, half):
        hrows = pl.ds(h0, half)
        t1_ref[hrows, :] = (jnp.dot(cb_ref[hrows, :], wout_ref[...], preferred_element_type=F32)
                            + bout_ref[...])

        def ln_out(off, rb, h0=h0):
            rows = pl.ds(h0 + off, rb)
            o_ref[rows, :] = _layer_norm(ALPHA * o_ref[rows, :] + t1_ref[rows, :],
                                         ln1g_ref[...], ln1b_ref[...])

        _for_row_blocks(half, ln_out)

    tail = xa_ref[m:m + hist, :]
    xa_ref[0:hist, :] = tail
    conv_ref[...] = tail


def _whole(shape):
    return pl.BlockSpec(memory_space=pltpu.VMEM)


def _mixer(x, conv0, h0, sre0, sim0, params, *, nb, tc):
    m = nb * tc
    row_block = pl.BlockSpec((m, D_MODEL), lambda i: (i, 0))
    if x.ndim == 3:
        assert x.shape[0] == nb and x.shape[1] % tc == 0
        m_total = nb * x.shape[1]
        x_spec = pl.BlockSpec((nb, tc, D_MODEL), lambda i: (0, i, 0))
    else:
        m_total = x.shape[0]
        x_spec = row_block
    assert m_total % m == 0
    hist = (LRU_CONV - 1) * nb
    in_specs = [x_spec] + [_whole(None)] * (4 + len(params))
    out_shape = (jax.ShapeDtypeStruct((m_total, D_MODEL), F32),
                 jax.ShapeDtypeStruct((hist, D_LRU), F32),
                 jax.ShapeDtypeStruct((nb, D_LRU), F32),
                 jax.ShapeDtypeStruct((nb, N_STATE), F32),
                 jax.ShapeDtypeStruct((nb, N_STATE), F32))
    out_specs = (row_block,
                 pl.BlockSpec((hist, D_LRU), lambda i: (0, 0)),
                 pl.BlockSpec((nb, D_LRU), lambda i: (0, 0)),
                 pl.BlockSpec((nb, N_STATE), lambda i: (0, 0)),
                 pl.BlockSpec((nb, N_STATE), lambda i: (0, 0)))
    scratch = [
        pltpu.VMEM((m, D_MODEL), BF16),
        pltpu.VMEM((hist + m, D_LRU), F32),
        pltpu.VMEM((m, D_SSM), F32),
        pltpu.VMEM((m, D_MODEL), F32),
        pltpu.VMEM((m, D_MODEL), F32),
        pltpu.VMEM((m, D_MODEL), F32),
        pltpu.VMEM((m, D_MODEL), BF16),
        pltpu.VMEM((m, N_STATE), F32),
        pltpu.VMEM((m, N_STATE), F32),
        pltpu.VMEM((m, D_LRU), F32),
        pltpu.VMEM((m, D_SSM), F32),
        pltpu.VMEM((m, D_SSM), BF16),
    ]
    return pl.pallas_call(
        functools.partial(_mixer_kernel, nb, tc),
        grid=(m_total // m,),
        in_specs=in_specs,
        out_specs=out_specs,
        out_shape=out_shape,
        scratch_shapes=scratch,
        compiler_params=pltpu.CompilerParams(
            dimension_semantics=("arbitrary",), vmem_limit_bytes=VMEM_LIMIT_BYTES),
        name=f"mixer_nb{nb}_tc{tc}",
    )(x, conv0, h0, sre0, sim0, *params)


def _ffn_kernel(nb, tc,
                x_ref, g0_ref, wup_ref, bup_ref, convw_ref, convb_ref, wdown_ref, bdown_ref,
                ln2g_ref, ln2b_ref,
                o_ref, gnew_ref,
                xb_ref, g_ref, v_ref, hb_ref, t_ref):
    m = nb * tc
    hist = (FFN_CONV - 1) * nb
    step = pl.program_id(0)
    rb = min(ROW_BLOCK, m)
    row_blocks = [pl.ds(off, rb) for off in range(0, m, rb)]

    @pl.when(step == 0)
    def _():
        g_ref[0:hist, :] = g0_ref[...]

    for rows in row_blocks:
        xb_ref[rows, :] = x_ref[rows, :].astype(BF16)

    for s in range(D_FF // FFN_SLAB):
        cols = slice(s * FFN_SLAB, (s + 1) * FFN_SLAB)
        vcols = slice(D_FF + s * FFN_SLAB, D_FF + (s + 1) * FFN_SLAB)
        g_ref[hist:hist + m, cols] = (
            jnp.dot(xb_ref[...], wup_ref[:, cols], preferred_element_type=F32)
            + bup_ref[:, cols])
        v_ref[:, cols] = (
            jnp.dot(xb_ref[...], wup_ref[:, vcols], preferred_element_type=F32)
            + bup_ref[:, vcols])
        for rows in row_blocks:
            acc = convb_ref[:, cols] + convw_ref[0:1, cols] * g_ref[rows, cols]
            for j in range(1, FFN_CONV):
                shifted = pl.ds(rows.start + j * nb, rb)
                acc = acc + convw_ref[j:j + 1, cols] * g_ref[shifted, cols]
            hb_ref[rows, cols] = (_gelu(acc) * v_ref[rows, cols]).astype(BF16)
        part = jnp.dot(hb_ref[:, cols], wdown_ref[cols, :], preferred_element_type=F32)
        if s == 0:
            t_ref[...] = part + bdown_ref[...]
        else:
            t_ref[...] += part

    for rows in row_blocks:
        out = _layer_norm(ALPHA * x_ref[rows, :] + t_ref[rows, :], ln2g_ref[...], ln2b_ref[...])
        if len(o_ref.shape) == 3:
            for k in range(rb // nb):
                o_ref[:, rows.start // nb + k, :] = out[k * nb:(k + 1) * nb, :]
        else:
            o_ref[rows, :] = out

    tail = g_ref[m:m + hist, :]
    g_ref[0:hist, :] = tail
    gnew_ref[...] = tail


def _ffn(x, g0, params, *, nb, tc, batch_major_out=False):
    m_total = x.shape[0]
    m = nb * tc
    assert m_total % m == 0
    hist = (FFN_CONV - 1) * nb
    row_block = pl.BlockSpec((m, D_MODEL), lambda i: (i, 0))
    in_specs = [row_block] + [_whole(None)] * (1 + len(params))
    if batch_major_out:
        y_shape = jax.ShapeDtypeStruct((nb, m_total // nb, D_MODEL), F32)
        y_spec = pl.BlockSpec((nb, tc, D_MODEL), lambda i: (0, i, 0))
    else:
        y_shape = jax.ShapeDtypeStruct((m_total, D_MODEL), F32)
        y_spec = row_block
    out_shape = (y_shape, jax.ShapeDtypeStruct((hist, D_FF), F32))
    out_specs = (y_spec, pl.BlockSpec((hist, D_FF), lambda i: (0, 0)))
    scratch = [
        pltpu.VMEM((m, D_MODEL), BF16),
        pltpu.VMEM((hist + m, D_FF), F32),
        pltpu.VMEM((m, D_FF), F32),
        pltpu.VMEM((m, D_FF), BF16),
        pltpu.VMEM((m, D_MODEL), F32),
    ]
    return pl.pallas_call(
        functools.partial(_ffn_kernel, nb, tc),
        grid=(m_total // m,),
        in_specs=in_specs,
        out_specs=out_specs,
        out_shape=out_shape,
        scratch_shapes=scratch,
        compiler_params=pltpu.CompilerParams(
            dimension_semantics=("arbitrary",), vmem_limit_bytes=VMEM_LIMIT_BYTES),
        name=f"ffn_nb{nb}_tc{tc}",
    )(x, g0, *params)


def _block_diag_mask(rows_per_group, cols_per_group):
    r = jnp.arange(SSM_GROUPS * rows_per_group)[:, None] // rows_per_group
    c = jnp.arange(SSM_GROUPS * cols_per_group)[None, :] // cols_per_group
    return (r == c).astype(F32)


def _ssm_matrices(bb_re_t, bb_im_t, c_re, c_im):
    mask_b = _block_diag_mask(SSM_GROUP, SSM_STATE)
    wb_re = jnp.tile(bb_re_t, (SSM_GROUPS, 1)) * mask_b
    wb_im = jnp.tile(bb_im_t, (SSM_GROUPS, 1)) * mask_b
    kb = D_SSM // SSM_SLABS
    nbk = N_STATE // SSM_SLABS
    wb = jnp.stack([
        jnp.concatenate([wb_re[s * kb:(s + 1) * kb, s * nbk:(s + 1) * nbk],
                         wb_im[s * kb:(s + 1) * kb, s * nbk:(s + 1) * nbk]], axis=1)
        for s in range(SSM_SLABS)]).astype(BF16)

    mask_c = _block_diag_mask(SSM_STATE, SSM_GROUP)

    def c_dense(c):
        ct = c.transpose(0, 2, 1).reshape(N_STATE, SSM_GROUP)
        return jnp.tile(ct, (1, SSM_GROUPS)) * mask_c

    def c_slabs(c):
        cd = c_dense(c)
        return jnp.stack([cd[s * nbk:(s + 1) * nbk, s * kb:(s + 1) * kb]
                          for s in range(SSM_SLABS)]).astype(BF16)

    return wb, c_slabs(c_re), c_slabs(c_im)


def _to_time_major(state):
    nb, k, c = state.shape
    return state.transpose(1, 0, 2).reshape(k * nb, c)


def _from_time_major(state, nb):
    k = state.shape[0] // nb
    return state.reshape(k, nb, state.shape[1]).transpose(1, 0, 2)


def kernel(x_prompt, x_sample, cache_conv_lru, state_lru, state_ssm_re, state_ssm_im, cache_conv_ffn,
           meta_tokens, emb_ln_g, emb_ln_b, w_in, b_in, conv_lru_w, conv_lru_b,
           lru_w_r, lru_b_r, lru_w_i, lru_b_i, lru_lambda,
           ssm_lambda_re, ssm_lambda_im, ssm_log_dt, ssm_b_re, ssm_b_im, ssm_c_re, ssm_c_im, ssm_d,
           w_glu, b_glu, w_out, b_out, ln1_g, ln1_b,
           w_up, b_up, ffn_conv_w, ffn_conv_b, w_down, b_down, ln2_g, ln2_b):
    assert w_in.shape[0] == DEPTH
    nbp, seq, _ = x_prompt.shape
    nbs = x_sample.shape[0]
    row = lambda a: a.reshape(1, -1).astype(F32)

    ab_re, ab_im, bb_re_t, bb_im_t = _ssm_prep(ssm_lambda_re[0], ssm_lambda_im[0], ssm_log_dt[0],
                                                ssm_b_re[0], ssm_b_im[0])
    wb, wc_re, wc_im = _ssm_matrices(bb_re_t, bb_im_t, ssm_c_re[0], ssm_c_im[0])
    wri = jnp.concatenate([lru_w_r[0], lru_w_i[0]], axis=-1).astype(BF16)

    mixer_params = (row(emb_ln_g), row(emb_ln_b), w_in[0].astype(BF16), row(b_in[0]),
                    conv_lru_w[0], row(conv_lru_b[0]),
                    wri, row(lru_b_r[0]), row(lru_b_i[0]), row(lru_lambda[0]),
                    ab_re, ab_im, wb, wc_re, wc_im, row(ssm_d[0]),
                    w_glu[0].astype(BF16), row(b_glu[0]), w_out[0].astype(BF16), row(b_out[0]),
                    row(ln1_g[0]), row(ln1_b[0]))
    ffn_params = (w_up[0].astype(BF16), row(b_up[0]), ffn_conv_w[0], row(ffn_conv_b[0]),
                  w_down[0].astype(BF16), row(b_down[0]), row(ln2_g[0]), row(ln2_b[0]))

    def layer(x, conv0, h0, sre0, sim0, g0, nb, tc):
        x1, conv, h, sre, sim = _mixer(x, conv0, h0, sre0, sim0, mixer_params, nb=nb, tc=tc)
        x2, g = _ffn(x1, g0, ffn_params, nb=nb, tc=tc, batch_major_out=(x.ndim == 3))
        return x2, (conv, h, sre, sim, g)

    zeros = lambda r, c: jnp.zeros((r, c), F32)
    meta = jnp.broadcast_to(meta_tokens[:, None, :].astype(F32), (N_META, nbp, D_MODEL))
    _, st = layer(meta.reshape(N_META * nbp, D_MODEL),
                  zeros((LRU_CONV - 1) * nbp, D_LRU), zeros(nbp, D_LRU),
                  zeros(nbp, N_STATE), zeros(nbp, N_STATE), zeros((FFN_CONV - 1) * nbp, D_FF),
                  nbp, N_META)
    y_prompt, st_p = layer(x_prompt, *st, nbp, PROMPT_STEPS_PER_CHUNK)

    ys, st_s = layer(x_sample.reshape(nbs, D_MODEL),
                     _to_time_major(cache_conv_lru[0]), state_lru[0],
                     state_ssm_re[0].reshape(nbs, N_STATE), state_ssm_im[0].reshape(nbs, N_STATE),
                     _to_time_major(cache_conv_ffn[0]), nbs, 1)
    y_sample = ys.reshape(nbs, 1, D_MODEL)

    def states(st, nb):
        conv, h, sre, sim, g = st
        return (_from_time_major(conv, nb)[None], h[None],
                sre.reshape(1, nb, SSM_GROUPS, SSM_STATE), sim.reshape(1, nb, SSM_GROUPS, SSM_STATE),
                _from_time_major(g, nb)[None])

    return (y_prompt, y_sample) + states(st_p, nbp) + states(st_s, nbs)
```

```python
import functools
import math

import jax
import jax.numpy as jnp
from jax import lax
from jax.experimental import pallas as pl
from jax.experimental.pallas import tpu as pltpu

D_MODEL = 1024
N_META = 16
D_LRU = D_MODEL
LRU_BLOCKS = 8
LRU_BLOCK = D_LRU // LRU_BLOCKS
LRU_CONV = 4
LRU_C = 8.0
D_SSM = D_MODEL // 2
SSM_GROUP = 16
SSM_GROUPS = D_SSM // SSM_GROUP
SSM_STATE = 64
N_STATE = SSM_GROUPS * SSM_STATE
D_FF = 3 * D_MODEL
FFN_CONV = 3
D_IN = D_LRU + D_SSM + 2 * D_MODEL
DEPTH = 1
ALPHA = (2.0 * DEPTH) ** 0.25
LN_EPS = 1e-5
GELU_C = math.sqrt(2.0 / math.pi)

V7X_VMEM_BYTES = 64 * 1024 * 1024
VMEM_LIMIT_BYTES = V7X_VMEM_BYTES - 8 * 1024 * 1024
SUBLANES = 8
ROW_BLOCK = 512
SSM_SLABS = 2
SSM_LANE_BLOCK = 512
PROMPT_STEPS_PER_CHUNK = 64
FFN_SLAB = 1536

BF16 = jnp.bfloat16
F32 = jnp.float32


def _layer_norm(x, g, b):
    mu = jnp.mean(x, axis=-1, keepdims=True)
    xc = x - mu
    var = jnp.mean(xc * xc, axis=-1, keepdims=True)
    return xc * lax.rsqrt(var + LN_EPS) * g + b


def _gelu(x):
    return 0.5 * x * (1.0 + jnp.tanh(GELU_C * (x + 0.044715 * (x * x * x))))


def _sigmoid(x):
    return 0.5 * jnp.tanh(0.5 * x) + 0.5


def _softplus(x):
    return jnp.maximum(x, 0.0) + jnp.log1p(jnp.exp(-jnp.abs(x)))


def _for_row_blocks(n_rows, fn):
    rb = min(ROW_BLOCK, n_rows)
    assert n_rows % rb == 0
    for off in range(0, n_rows, rb):
        fn(off, rb)


def _ssm_prep_kernel(lre_ref, lim_ref, logdt_ref, bre_ref, bim_ref,
                     abre_ref, abim_ref, bbre_ref, bbim_ref):
    lre = lre_ref[...]
    lim = lim_ref[...]
    dt = jnp.exp(logdt_ref[...])
    mag = jnp.exp(lre * dt)
    ab_re = mag * jnp.cos(lim * dt)
    ab_im = mag * jnp.sin(lim * dt)
    den = lre * lre + lim * lim
    nr = ab_re - 1.0
    ni = ab_im
    f_re = (nr * lre + ni * lim) / den
    f_im = (ni * lre - nr * lim) / den
    br = bre_ref[...]
    bi = bim_ref[...]
    abre_ref[...] = ab_re
    abim_ref[...] = ab_im
    bbre_ref[...] = f_re * br - f_im * bi
    bbim_ref[...] = f_re * bi + f_im * br


def _ssm_prep(lam_re, lam_im, log_dt, b_re, b_im):
    row = lambda a: a.reshape(1, N_STATE)
    b_t = lambda a: a.transpose(2, 0, 1).reshape(SSM_GROUP, N_STATE)
    logdt = jnp.repeat(log_dt, SSM_STATE).reshape(1, N_STATE)
    out_shape = (jax.ShapeDtypeStruct((1, N_STATE), F32),) * 2 + (
        jax.ShapeDtypeStruct((SSM_GROUP, N_STATE), F32),) * 2
    return pl.pallas_call(_ssm_prep_kernel, out_shape=out_shape, name="ssm_prep")(
        row(lam_re), row(lam_im), logdt, b_t(b_re), b_t(b_im))


def _mixer_kernel(nb, tc,
                  x_ref, conv0_ref, h0_ref, sre0_ref, sim0_ref,
                  embg_ref, embb_ref, win_ref, bin_ref, convw_ref, convb_ref,
                  wri_ref, br_ref, bi_ref, lam_ref,
                  abre_ref, abim_ref, wb_ref, wcre_ref, wcim_ref, ssmd_ref,
                  wglu_ref, bglu_ref, wout_ref, bout_ref, ln1g_ref, ln1b_ref,
                  o_ref, conv_ref, h_ref, sre_ref, sim_ref,
                  zb_ref, xa_ref, us_ref, ga_ref, gb_ref, t1_ref, cb_ref, p_ref, q_ref,
                  ya_ref, y_ref, glb_ref):
    m = nb * tc
    hist = (LRU_CONV - 1) * nb
    step = pl.program_id(0)

    @pl.when(step == 0)
    def _():
        xa_ref[0:hist, :] = conv0_ref[...]
        h_ref[...] = h0_ref[...]
        sre_ref[...] = sre0_ref[...]
        sim_ref[...] = sim0_ref[...]

    if len(x_ref.shape) == 3:
        for t in range(tc):
            o_ref[pl.ds(t * nb, nb), :] = x_ref[:, t, :]
        src_ref = o_ref
    else:
        src_ref = x_ref

    def ln_in(off, rb):
        z = _layer_norm(src_ref[pl.ds(off, rb), :], embg_ref[...], embb_ref[...])
        o_ref[pl.ds(off, rb), :] = z
        zb_ref[pl.ds(off, rb), :] = z.astype(BF16)

    _for_row_blocks(m, ln_in)

    def proj(lo, hi):
        return (jnp.dot(zb_ref[...], win_ref[:, lo:hi], preferred_element_type=F32)
                + bin_ref[:, lo:hi])

    c0, c1, c2 = D_LRU, D_LRU + D_SSM, D_LRU + D_SSM + D_MODEL
    us_ref[...] = proj(c0, c1)

    kb = D_SSM // SSM_SLABS
    nbk = N_STATE // SSM_SLABS
    for s in range(SSM_SLABS):
        bu = jnp.dot(us_ref[:, s * kb:(s + 1) * kb].astype(BF16), wb_ref[s],
                     preferred_element_type=F32)
        p_ref[:, s * nbk:(s + 1) * nbk] = bu[:, :nbk]
        q_ref[:, s * nbk:(s + 1) * nbk] = bu[:, nbk:]

    xa_ref[hist:hist + m, :] = proj(0, c0)

    if tc == 1:
        def ssm_step(off, rb):
            rows = pl.ds(off, rb)
            ar, ai = abre_ref[...], abim_ref[...]
            hr, hi = sre0_ref[rows, :], sim0_ref[rows, :]
            nr = ar * hr - ai * hi + p_ref[rows, :]
            ni = ar * hi + ai * hr + q_ref[rows, :]
            p_ref[rows, :] = nr
            q_ref[rows, :] = ni
            sre_ref[rows, :] = nr
            sim_ref[rows, :] = ni

        for off in range(0, m, SUBLANES):
            ssm_step(off, SUBLANES)
    else:
        assert nb == SUBLANES
        for lb in range(N_STATE // SSM_LANE_BLOCK):
            lanes = slice(lb * SSM_LANE_BLOCK, (lb + 1) * SSM_LANE_BLOCK)
            ar = jnp.broadcast_to(abre_ref[:, lanes], (nb, SSM_LANE_BLOCK))
            ai = jnp.broadcast_to(abim_ref[:, lanes], (nb, SSM_LANE_BLOCK))
            hr, hi = sre_ref[:, lanes], sim_ref[:, lanes]
            for t in range(tc):
                rows = pl.ds(t * nb, nb)
                hr, hi = (ar * hr - ai * hi + p_ref[rows, lanes],
                          ar * hi + ai * hr + q_ref[rows, lanes])
                p_ref[rows, lanes] = hr
                q_ref[rows, lanes] = hi
            sre_ref[:, lanes] = hr
            sim_ref[:, lanes] = hi

    def conv(off, rb):
        acc = convb_ref[...] + convw_ref[0:1, :] * xa_ref[pl.ds(off, rb), :]
        for j in range(1, LRU_CONV):
            acc = acc + convw_ref[j:j + 1, :] * xa_ref[pl.ds(off + j * nb, rb), :]
        t1_ref[pl.ds(off, rb), :] = acc
        cb_ref[pl.ds(off, rb), :] = acc.astype(BF16)

    _for_row_blocks(m, conv)

    kc = D_SSM // SSM_SLABS
    for s in range(SSM_SLABS):
        hs = slice(s * nbk, (s + 1) * nbk)
        y_ref[:, s * kc:(s + 1) * kc] = (
            jnp.dot(p_ref[:, hs].astype(BF16), wcre_ref[s], preferred_element_type=F32)
            - jnp.dot(q_ref[:, hs].astype(BF16), wcim_ref[s], preferred_element_type=F32))

    def ssm_out(off, rb):
        rows = pl.ds(off, rb)
        ys = y_ref[rows, :] + ssmd_ref[...] * us_ref[rows, :]
        glb_ref[rows, :] = _gelu(ys).astype(BF16)

    _for_row_blocks(m, ssm_out)

    for n in range(LRU_BLOCKS):
        q_ref[:, 2 * n * LRU_BLOCK:(2 * n + 2) * LRU_BLOCK] = jnp.dot(
            cb_ref[:, n * LRU_BLOCK:(n + 1) * LRU_BLOCK], wri_ref[n],
            preferred_element_type=F32)
    ga_ref[...] = proj(c1, c2)
    p_ref[...] = (jnp.dot(glb_ref[...], wglu_ref[...], preferred_element_type=F32)
                  + bglu_ref[...])
    gb_ref[...] = proj(c2, D_IN)

    sp = _softplus(-lam_ref[...])
    rb = min(ROW_BLOCK, m)
    for n in range(LRU_BLOCKS):
        cols = slice(n * LRU_BLOCK, (n + 1) * LRU_BLOCK)
        rcols = slice(2 * n * LRU_BLOCK, (2 * n + 1) * LRU_BLOCK)
        icols = slice((2 * n + 1) * LRU_BLOCK, (2 * n + 2) * LRU_BLOCK)
        h = None if tc == 1 else h_ref[:, cols]
        for off in range(0, m, rb):
            rows = pl.ds(off, rb)
            r = _sigmoid(q_ref[rows, rcols] + br_ref[:, cols])
            i = _sigmoid(q_ref[rows, icols] + bi_ref[:, cols])
            log_a = (-LRU_C * r) * sp[:, cols]
            a = jnp.exp(log_a)
            mult = jnp.sqrt(-jnp.tanh(log_a) * (a * a + 1.0))
            bx = mult * (i * t1_ref[rows, cols])
            if tc == 1:
                hn = a * h0_ref[rows, cols] + bx
                ya_ref[rows, cols] = hn
                h_ref[rows, cols] = hn
            else:
                for k in range(0, rb, nb):
                    h = a[k:k + nb, :] * h + bx[k:k + nb, :]
                    ya_ref[pl.ds(off + k, nb), cols] = h
        if tc != 1:
            h_ref[:, cols] = h

    def merge(off, rb):
        rows = pl.ds(off, rb)
        yb = p_ref[rows, 0:D_MODEL] * _sigmoid(p_ref[rows, D_MODEL:2 * D_MODEL])
        mix = _sigmoid(ga_ref[rows, :]) * ya_ref[rows, :] + _sigmoid(gb_ref[rows, :]) * yb
        cb_ref[rows, :] = mix.astype(BF16)

    _for_row_blocks(m, merge)

    half = m // 2
    for h0 in (0, half):
        hrows = pl.ds(h0, half)
        t1_ref[hrows, :] = (jnp.dot(cb_ref[hrows, :], wout_ref[...], preferred_element_type=F32)
                            + bout_ref[...])

        def ln_out(off, rb, h0=h0):
            rows = pl.ds(h0 + off, rb)
            o_ref[rows, :] = _layer_norm(ALPHA * o_ref[rows, :] + t1_ref[rows, :],
                                         ln1g_ref[...], ln1b_ref[...])

        _for_row_blocks(half, ln_out)

    tail = xa_ref[m:m + hist, :]
    xa_ref[0:hist, :] = tail
    conv_ref[...] = tail


def _whole(shape):
    return pl.BlockSpec(memory_space=pltpu.VMEM)


def _mixer(x, conv0, h0, sre0, sim0, params, *, nb, tc):
    m = nb * tc
    row_block = pl.BlockSpec((m, D_MODEL), lambda i: (i, 0))
    if x.ndim == 3:
        assert x.shape[0] == nb and x.shape[1] % tc == 0
        m_total = nb * x.shape[1]
        x_spec = pl.BlockSpec((nb, tc, D_MODEL), lambda i: (0, i, 0))
    else:
        m_total = x.shape[0]
        x_spec = row_block
    assert m_total % m == 0
    hist = (LRU_CONV - 1) * nb
    in_specs = [x_spec] + [_whole(None)] * (4 + len(params))
    out_shape = (jax.ShapeDtypeStruct((m_total, D_MODEL), F32),
                 jax.ShapeDtypeStruct((hist, D_LRU), F32),
                 jax.ShapeDtypeStruct((nb, D_LRU), F32),
                 jax.ShapeDtypeStruct((nb, N_STATE), F32),
                 jax.ShapeDtypeStruct((nb, N_STATE), F32))
    out_specs = (row_block,
                 pl.BlockSpec((hist, D_LRU), lambda i: (0, 0)),
                 pl.BlockSpec((nb, D_LRU), lambda i: (0, 0)),
                 pl.BlockSpec((nb, N_STATE), lambda i: (0, 0)),
                 pl.BlockSpec((nb, N_STATE), lambda i: (0, 0)))
    scratch = [
        pltpu.VMEM((m, D_MODEL), BF16),
        pltpu.VMEM((hist + m, D_LRU), F32),
        pltpu.VMEM((m, D_SSM), F32),
        pltpu.VMEM((m, D_MODEL), F32),
        pltpu.VMEM((m, D_MODEL), F32),
        pltpu.VMEM((m, D_MODEL), F32),
        pltpu.VMEM((m, D_MODEL), BF16),
        pltpu.VMEM((m, N_STATE), F32),
        pltpu.VMEM((m, N_STATE), F32),
        pltpu.VMEM((m, D_LRU), F32),
        pltpu.VMEM((m, D_SSM), F32),
        pltpu.VMEM((m, D_SSM), BF16),
    ]
    return pl.pallas_call(
        functools.partial(_mixer_kernel, nb, tc),
        grid=(m_total // m,),
        in_specs=in_specs,
        out_specs=out_specs,
        out_shape=out_shape,
        scratch_shapes=scratch,
        compiler_params=pltpu.CompilerParams(
            dimension_semantics=("arbitrary",), vmem_limit_bytes=VMEM_LIMIT_BYTES),
        name=f"mixer_nb{nb}_tc{tc}",
    )(x, conv0, h0, sre0, sim0, *params)


def _ffn_kernel(nb, tc,
                x_ref, g0_ref, wup_ref, bup_ref, convw_ref, convb_ref, wdown_ref, bdown_ref,
                ln2g_ref, ln2b_ref,
                o_ref, gnew_ref,
                xb_ref, g_ref, v_ref, hb_ref, t_ref):
    m = nb * tc
    hist = (FFN_CONV - 1) * nb
    step = pl.program_id(0)
    rb = min(ROW_BLOCK, m)
    row_blocks = [pl.ds(off, rb) for off in range(0, m, rb)]

    @pl.when(step == 0)
    def _():
        g_ref[0:hist, :] = g0_ref[...]

    for rows in row_blocks:
        xb_ref[rows, :] = x_ref[rows, :].astype(BF16)

    for s in range(D_FF // FFN_SLAB):
        cols = slice(s * FFN_SLAB, (s + 1) * FFN_SLAB)
        vcols = slice(D_FF + s * FFN_SLAB, D_FF + (s + 1) * FFN_SLAB)
        g_ref[hist:hist + m, cols] = (
            jnp.dot(xb_ref[...], wup_ref[:, cols], preferred_element_type=F32)
            + bup_ref[:, cols])
        v_ref[:, cols] = (
            jnp.dot(xb_ref[...], wup_ref[:, vcols], preferred_element_type=F32)
            + bup_ref[:, vcols])
        for rows in row_blocks:
            acc = convb_ref[:, cols] + convw_ref[0:1, cols] * g_ref[rows, cols]
            for j in range(1, FFN_CONV):
                shifted = pl.ds(rows.start + j * nb, rb)
                acc = acc + convw_ref[j:j + 1, cols] * g_ref[shifted, cols]
            hb_ref[rows, cols] = (_gelu(acc) * v_ref[rows, cols]).astype(BF16)
        part = jnp.dot(hb_ref[:, cols], wdown_ref[cols, :], preferred_element_type=F32)
        if s == 0:
            t_ref[...] = part + bdown_ref[...]
        else:
            t_ref[...] += part

    for rows in row_blocks:
        out = _layer_norm(ALPHA * x_ref[rows, :] + t_ref[rows, :], ln2g_ref[...], ln2b_ref[...])
        if len(o_ref.shape) == 3:
            for k in range(rb // nb):
                o_ref[:, rows.start // nb + k, :] = out[k * nb:(k + 1) * nb, :]
        else:
            o_ref[rows, :] = out

    tail = g_ref[m:m + hist, :]
    g_ref[0:hist, :] = tail
    gnew_ref[...] = tail


def _ffn(x, g0, params, *, nb, tc, batch_major_out=False):
    m_total = x.shape[0]
    m = nb * tc
    assert m_total % m == 0
    hist = (FFN_CONV - 1) * nb
    row_block = pl.BlockSpec((m, D_MODEL), lambda i: (i, 0))
    in_specs = [row_block] + [_whole(None)] * (1 + len(params))
    if batch_major_out:
        y_shape = jax.ShapeDtypeStruct((nb, m_total // nb, D_MODEL), F32)
        y_spec = pl.BlockSpec((nb, tc, D_MODEL), lambda i: (0, i, 0))
    else:
        y_shape = jax.ShapeDtypeStruct((m_total, D_MODEL), F32)
        y_spec = row_block
    out_shape = (y_shape, jax.ShapeDtypeStruct((hist, D_FF), F32))
    out_specs = (y_spec, pl.BlockSpec((hist, D_FF), lambda i: (0, 0)))
    scratch = [
        pltpu.VMEM((m, D_MODEL), BF16),
        pltpu.VMEM((hist + m, D_FF), F32),
        pltpu.VMEM((m, D_FF), F32),
        pltpu.VMEM((m, D_FF), BF16),
        pltpu.VMEM((m, D_MODEL), F32),
    ]
    return pl.pallas_call(
        functools.partial(_ffn_kernel, nb, tc),
        grid=(m_total // m,),
        in_specs=in_specs,
        out_specs=out_specs,
        out_shape=out_shape,
        scratch_shapes=scratch,
        compiler_params=pltpu.CompilerParams(
            dimension_semantics=("arbitrary",), vmem_limit_bytes=VMEM_LIMIT_BYTES),
        name=f"ffn_nb{nb}_tc{tc}",
    )(x, g0, *params)


def _block_diag_mask(rows_per_group, cols_per_group):
    r = jnp.arange(SSM_GROUPS * rows_per_group)[:, None] // rows_per_group
    c = jnp.arange(SSM_GROUPS * cols_per_group)[None, :] // cols_per_group
    return (r == c).astype(F32)


def _ssm_matrices(bb_re_t, bb_im_t, c_re, c_im):
    mask_b = _block_diag_mask(SSM_GROUP, SSM_STATE)
    wb_re = jnp.tile(bb_re_t, (SSM_GROUPS, 1)) * mask_b
    wb_im = jnp.tile(bb_im_t, (SSM_GROUPS, 1)) * mask_b
    kb = D_SSM // SSM_SLABS
    nbk = N_STATE // SSM_SLABS
    wb = jnp.stack([
        jnp.concatenate([wb_re[s * kb:(s + 1) * kb, s * nbk:(s + 1) * nbk],
                         wb_im[s * kb:(s + 1) * kb, s * nbk:(s + 1) * nbk]], axis=1)
        for s in range(SSM_SLABS)]).astype(BF16)

    mask_c = _block_diag_mask(SSM_STATE, SSM_GROUP)

    def c_dense(c):
        ct = c.transpose(0, 2, 1).reshape(N_STATE, SSM_GROUP)
        return jnp.tile(ct, (1, SSM_GROUPS)) * mask_c

    def c_slabs(c):
        cd = c_dense(c)
        return jnp.stack([cd[s * nbk:(s + 1) * nbk, s * kb:(s + 1) * kb]
                          for s in range(SSM_SLABS)]).astype(BF16)

    return wb, c_slabs(c_re), c_slabs(c_im)


def _to_time_major(state):
    nb, k, c = state.shape
    return state.transpose(1, 0, 2).reshape(k * nb, c)


def _from_time_major(state, nb):
    k = state.shape[0] // nb
    return state.reshape(k, nb, state.shape[1]).transpose(1, 0, 2)


def kernel(x_prompt, x_sample, cache_conv_lru, state_lru, state_ssm_re, state_ssm_im, cache_conv_ffn,
           meta_tokens, emb_ln_g, emb_ln_b, w_in, b_in, conv_lru_w, conv_lru_b,
           lru_w_r, lru_b_r, lru_w_i, lru_b_i, lru_lambda,
           ssm_lambda_re, ssm_lambda_im, ssm_log_dt, ssm_b_re, ssm_b_im, ssm_c_re, ssm_c_im, ssm_d,
           w_glu, b_glu, w_out, b_out, ln1_g, ln1_b,
           w_up, b_up, ffn_conv_w, ffn_conv_b, w_down, b_down, ln2_g, ln2_b):
    assert w_in.shape[0] == DEPTH
    nbp, seq, _ = x_prompt.shape
    nbs = x_sample.shape[0]
    row = lambda a: a.reshape(1, -1).astype(F32)

    ab_re, ab_im, bb_re_t, bb_im_t = _ssm_prep(ssm_lambda_re[0], ssm_lambda_im[0], ssm_log_dt[0],
                                                ssm_b_re[0], ssm_b_im[0])
    wb, wc_re, wc_im = _ssm_matrices(bb_re_t, bb_im_t, ssm_c_re[0], ssm_c_im[0])
    wri = jnp.concatenate([lru_w_r[0], lru_w_i[0]], axis=-1).astype(BF16)

    mixer_params = (row(emb_ln_g), row(emb_ln_b), w_in[0].astype(BF16), row(b_in[0]),
                    conv_lru_w[0], row(conv_lru_b[0]),
                    wri, row(lru_b_r[0]), row(lru_b_i[0]), row(lru_lambda[0]),
                    ab_re, ab_im, wb, wc_re, wc_im, row(ssm_d[0]),
                    w_glu[0].astype(BF16), row(b_glu[0]), w_out[0].astype(BF16), row(b_out[0]),
                    row(ln1_g[0]), row(ln1_b[0]))
    ffn_params = (w_up[0].astype(BF16), row(b_up[0]), ffn_conv_w[0], row(ffn_conv_b[0]),
                  w_down[0].astype(BF16), row(b_down[0]), row(ln2_g[0]), row(ln2_b[0]))

    def layer(x, conv0, h0, sre0, sim0, g0, nb, tc):
        x1, conv, h, sre, sim = _mixer(x, conv0, h0, sre0, sim0, mixer_params, nb=nb, tc=tc)
        x2, g = _ffn(x1, g0, ffn_params, nb=nb, tc=tc, batch_major_out=(x.ndim == 3))
        return x2, (conv, h, sre, sim, g)

    zeros = lambda r, c: jnp.zeros((r, c), F32)
    meta = jnp.broadcast_to(meta_tokens[:, None, :].astype(F32), (N_META, nbp, D_MODEL))
    _, st = layer(meta.reshape(N_META * nbp, D_MODEL),
                  zeros((LRU_CONV - 1) * nbp, D_LRU), zeros(nbp, D_LRU),
                  zeros(nbp, N_STATE), zeros(nbp, N_STATE), zeros((FFN_CONV - 1) * nbp, D_FF),
                  nbp, N_META)
    y_prompt, st_p = layer(x_prompt, *st, nbp, PROMPT_STEPS_PER_CHUNK)

    ys, st_s = layer(x_sample.reshape(nbs, D_MODEL),
                     _to_time_major(cache_conv_lru[0]), state_lru[0],
                     state_ssm_re[0].reshape(nbs, N_STATE), state_ssm_im[0].reshape(nbs, N_STATE),
                     _to_time_major(cache_conv_ffn[0]), nbs, 1)
    y_sample = ys.reshape(nbs, 1, D_MODEL)

    def states(st, nb):
        conv, h, sre, sim, g = st
        return (_from_time_major(conv, nb)[None], h[None],
                sre.reshape(1, nb, SSM_GROUPS, SSM_STATE), sim.reshape(1, nb, SSM_GROUPS, SSM_STATE),
                _from_time_major(g, nb)[None])

    return (y_prompt, y_sample) + states(st_p, nbp) + states(st_s, nbs)
```
